```python
import jax, jax.numpy as jnp
from jax import lax
import numpy as np

D_MODEL = 1024
BATCH = 8
SEQ = 8192
DEPTH = 1

CHUNK = 64
Q_BLOCK = 128
CONV_DIM = D_MODEL
CONV_WIDTH = 31
ATTN_HEADS = 16
HEAD_DIM = 64
ATTN_DIM = ATTN_HEADS * HEAD_DIM
N_BRANCHES = 2
IN_DIM = 2 * CONV_DIM + 3 * ATTN_DIM + N_BRANCHES * D_MODEL
N_EXPERTS = 32
TOP_K = 4
D_FF = D_MODEL
SWIGLU_LIMIT = 7.0
SWIGLU_ALPHA = 1.702
EPS = 1e-6

kernel_name = "hybrid_conformer_stickbreaking_moe_block"


def rms_norm(x, g):
    xf = x.astype(jnp.float32)
    y = xf * lax.rsqrt(jnp.mean(xf * xf, axis=-1, keepdims=True) + EPS)
    return (y * g.astype(jnp.float32)).astype(x.dtype)


def layer_norm(x, g, b):
    xf = x.astype(jnp.float32)
    mu = jnp.mean(xf, axis=-1, keepdims=True)
    var = jnp.mean(jnp.square(xf - mu), axis=-1, keepdims=True)
    y = (xf - mu) * lax.rsqrt(var + EPS)
    return (y * g.astype(jnp.float32) + b.astype(jnp.float32)).astype(x.dtype)


def conformer_conv(u2, dw_w, dw_b, norm_g, norm_b, w_proj):
    a, b = jnp.split(u2, 2, axis=-1)
    u = a * jax.nn.sigmoid(b)
    u = lax.conv_general_dilated(
        u, dw_w[:, None, :].astype(u.dtype), window_strides=(1,),
        padding=[(CONV_WIDTH - 1, 0)],
        dimension_numbers=('NWC', 'WIO', 'NWC'),
        feature_group_count=CONV_DIM) + dw_b
    u = jax.nn.silu(layer_norm(u, norm_g, norm_b))
    return u @ w_proj


def stick_breaking_attention(q, k, v):
    S = q.shape[2]
    scale = HEAD_DIM ** -0.5
    outs = []
    for i in range(S // Q_BLOCK):
        start, end = i * Q_BLOCK, (i + 1) * Q_BLOCK
        qb = q[:, :, start:end]
        kb = k[:, :, :end]
        vb = v[:, :, :end]
        z = jnp.einsum('bhqd,bhkd->bhqk', qb, kb).astype(jnp.float32) * scale
        q_pos = start + jnp.arange(Q_BLOCK)[:, None]
        k_pos = jnp.arange(end)[None, :]
        mask = k_pos < q_pos
        sp = jnp.where(mask, jax.nn.softplus(z), 0.0)
        later = lax.cumsum(sp, axis=3, reverse=True) - sp
        a = jnp.where(mask, jnp.exp(jax.nn.log_sigmoid(z) - later), 0.0)
        outs.append(jnp.einsum('bhqk,bhkd->bhqd', a.astype(v.dtype), vb))
    return jnp.concatenate(outs, axis=2)


def moe_ffn(h, router_w, router_b, w1, b1, w2, b2):
    B, S, D = h.shape
    t = h.reshape(-1, D)
    T = t.shape[0]
    logits = (t @ router_w).astype(jnp.float32) + router_b.astype(jnp.float32)
    top_v, top_i = lax.top_k(logits, TOP_K)
    gate = jax.nn.softmax(top_v, axis=-1)
    flat_e = top_i.reshape(-1)
    order = jnp.argsort(flat_e)
    sorted_e = flat_e[order]
    tok = order // TOP_K
    xs = t[tok]
    group_sizes = jnp.bincount(flat_e, length=N_EXPERTS).astype(jnp.int32)
    hid = lax.ragged_dot(xs, w1, group_sizes) + b1[sorted_e]
    glu, lin = jnp.split(hid, 2, axis=-1)
    glu = jnp.minimum(glu, SWIGLU_LIMIT)
    lin = jnp.clip(lin, -SWIGLU_LIMIT, SWIGLU_LIMIT)
    act = glu * jax.nn.sigmoid(SWIGLU_ALPHA * glu) * (lin + 1.0)
    out = lax.ragged_dot(act, w2, group_sizes) + b2[sorted_e]
    w = gate.reshape(-1)[order].astype(out.dtype)
    y = jax.ops.segment_sum(out * w[:, None], tok, num_segments=T)
    return y.reshape(B, S, D)


def setup_inputs(seed: int = 0) -> dict:
    key = jax.random.key(seed)
    ks = jax.random.split(key, 20)

    def nrm(k, shape, scale):
        return jax.random.normal(k, shape, jnp.float32) * scale

    return {
        "x": nrm(ks[0], (BATCH, SEQ, D_MODEL), 1.0),
        "norm1_g": 1.0 + nrm(ks[1], (DEPTH, D_MODEL), 0.02),
        "w_in": nrm(ks[2], (DEPTH, D_MODEL, IN_DIM), D_MODEL ** -0.5),
        "conv_dw_w": nrm(ks[3], (DEPTH, CONV_WIDTH, CONV_DIM), CONV_WIDTH ** -0.5),
        "conv_dw_b": nrm(ks[4], (DEPTH, CONV_DIM), 0.02),
        "conv_norm_g": 1.0 + nrm(ks[5], (DEPTH, CONV_DIM), 0.02),
        "conv_norm_b": nrm(ks[6], (DEPTH, CONV_DIM), 0.02),
        "w_conv_out": nrm(ks[7], (DEPTH, CONV_DIM, D_MODEL), CONV_DIM ** -0.5),
        "w_attn_out": nrm(ks[8], (DEPTH, ATTN_DIM, D_MODEL), ATTN_DIM ** -0.5),
        "gate_b": nrm(ks[9], (DEPTH, N_BRANCHES * D_MODEL), 0.02),
        "w_out": nrm(ks[10], (DEPTH, D_MODEL, D_MODEL), D_MODEL ** -0.5),
        "norm2_g": 1.0 + nrm(ks[11], (DEPTH, D_MODEL), 0.02),
        "router_w": nrm(ks[12], (DEPTH, D_MODEL, N_EXPERTS), D_MODEL ** -0.5),
        "router_b": nrm(ks[13], (DEPTH, N_EXPERTS), 0.01),
        "expert_w1": nrm(ks[14], (DEPTH, N_EXPERTS, D_MODEL, 2 * D_FF), D_MODEL ** -0.5),
        "expert_b1": nrm(ks[15], (DEPTH, N_EXPERTS, 2 * D_FF), 0.02),
        "expert_w2": nrm(ks[16], (DEPTH, N_EXPERTS, D_FF, D_MODEL), D_FF ** -0.5),
        "expert_b2": nrm(ks[17], (DEPTH, N_EXPERTS, D_MODEL), 0.02),
        "final_norm_g": 1.0 + nrm(ks[18], (D_MODEL,), 0.02),
    }


def reference(x, norm1_g, w_in, conv_dw_w, conv_dw_b, conv_norm_g, conv_norm_b, w_conv_out,
              w_attn_out, gate_b, w_out, norm2_g, router_w, router_b, expert_w1, expert_b1,
              expert_w2, expert_b2, final_norm_g):
    B, S, _ = x.shape
    for l in range(DEPTH):
        h = rms_norm(x, norm1_g[l])
        proj = h @ w_in[l]
        conv_in, qkv, gate_logits = jnp.split(
            proj, [2 * CONV_DIM, 2 * CONV_DIM + 3 * ATTN_DIM], axis=-1)
        y_conv = conformer_conv(conv_in, conv_dw_w[l], conv_dw_b[l], conv_norm_g[l],
                                conv_norm_b[l], w_conv_out[l])
        qkv = qkv.reshape(B, S, 3, ATTN_HEADS, HEAD_DIM)
        q = qkv[:, :, 0].transpose(0, 2, 1, 3)
        k = qkv[:, :, 1].transpose(0, 2, 1, 3)
        v = qkv[:, :, 2].transpose(0, 2, 1, 3)
        o = stick_breaking_attention(q, k, v).transpose(0, 2, 1, 3).reshape(B, S, ATTN_DIM)
        y_attn = o @ w_attn_out[l]
        g_conv, g_attn = jnp.split(jax.nn.sigmoid(gate_logits + gate_b[l]), 2, axis=-1)
        x = x + (g_conv * y_conv + g_attn * y_attn) @ w_out[l]
        h2 = rms_norm(x, norm2_g[l])
        x = x + moe_ffn(h2, router_w[l], router_b[l], expert_w1[l], expert_b1[l],
                        expert_w2[l], expert_b2[l])
    return rms_norm(x, final_norm_g)
```

```python
import functools

import jax
import jax.numpy as jnp
from jax import lax
from jax.experimental import pallas as pl
from jax.experimental.pallas import tpu as pltpu

F32 = jnp.float32
BF16 = jnp.bfloat16

HEAD_DIM = 64
CONV_WIDTH = 31
N_EXPERTS = 32
TOP_K = 4
SWIGLU_LIMIT = 7.0
SWIGLU_ALPHA = 1.702
EPS = 1e-6

LANES = 128
CONV_HALO = 32
VMEM_LIMIT = 56 * 1024 * 1024

SKIP_THRESHOLD = 104.0


def _params(sem):
    return pltpu.CompilerParams(dimension_semantics=sem, vmem_limit_bytes=VMEM_LIMIT)


def _resident(shape):
    return pl.BlockSpec(shape, lambda *_: (0,) * len(shape), pipeline_mode=pl.Buffered(1))


def _sigmoid(x):
    return 1.0 / (1.0 + jnp.exp(-x))


def _inproj_kernel(x_ref, g_ref, w_ref, gb_ref, u_ref, q_ref, k_ref, v_ref, gate_ref, *, d):
    x = x_ref[...]
    ms = jnp.mean(x * x, axis=-1, keepdims=True)
    h = (x * lax.rsqrt(ms + EPS) * g_ref[...]).astype(BF16)

    def proj(c):
        return jnp.dot(h, w_ref[:, c * d:(c + 1) * d], preferred_element_type=F32)

    u_ref[...] = (proj(0) * _sigmoid(proj(1))).astype(BF16)
    q_ref[...] = (proj(2) * (HEAD_DIM ** -0.5)).astype(BF16)
    k_ref[...] = proj(3).astype(BF16)
    v_ref[...] = proj(4).astype(BF16)
    gate_ref[:, :d] = _sigmoid(proj(5) + gb_ref[:, :d]).astype(BF16)
    gate_ref[:, d:] = _sigmoid(proj(6) + gb_ref[:, d:]).astype(BF16)


def _in_proj(x2, norm_g, w_in, gate_b, *, tm):
    t, d = x2.shape
    row = lambda n: pl.BlockSpec((tm, n), lambda i: (i, 0))
    out = lambda n: jax.ShapeDtypeStruct((t, n), BF16)
    return pl.pallas_call(
        functools.partial(_inproj_kernel, d=d),
        grid=(t // tm,),
        in_specs=[row(d), _resident((1, d)), _resident(w_in.shape), _resident((1, 2 * d))],
        out_specs=[row(d), row(d), row(d), row(d), row(2 * d)],
        out_shape=[out(d), out(d), out(d), out(d), out(2 * d)],
        compiler_params=_params(("parallel",)),
        name="in_proj",
    )(x2, norm_g, w_in, gate_b)


def _attn_kernel(q_ref, k_ref, v_ref, o_ref, *, tb):
    i = pl.program_id(2)
    r_iota = lax.broadcasted_iota(jnp.int32, (tb, tb), 0)
    c_iota = lax.broadcasted_iota(jnp.int32, (tb, tb), 1)
    upper = jnp.where(r_iota > c_iota, 1.0, 0.0).astype(BF16)
    causal = c_iota < r_iota

    def block(h, j, carry, acc, diagonal):
        lanes = slice(h * HEAD_DIM, (h + 1) * HEAD_DIM)
        start = pl.multiple_of(j * tb, tb)
        qh = q_ref[:, lanes]
        kh = k_ref[pl.ds(start, tb), lanes]
        vh = v_ref[pl.ds(start, tb), lanes]
        z = lax.dot_general(qh, kh, (((1,), (1,)), ((), ())), preferred_element_type=F32)
        l = jnp.log1p(jnp.exp(-jnp.abs(z)))
        sp = jnp.maximum(z, 0.0) + l
        logsig = jnp.minimum(z, 0.0) - l
        if diagonal:
            sp = jnp.where(causal, sp, 0.0)
        hi = sp.astype(BF16)
        lo = (sp - hi.astype(F32)).astype(BF16)
        later = (jnp.dot(hi, upper, preferred_element_type=F32)
                 + jnp.dot(lo, upper, preferred_element_type=F32) + carry)
        a = jnp.exp(logsig - later)
        if diagonal:
            a = jnp.where(causal, a, 0.0)
        acc = acc + jnp.dot(a.astype(BF16), vh, preferred_element_type=F32)
        carry = carry + jnp.sum(sp, axis=1, keepdims=True)
        return carry, acc

    for h in range(LANES // HEAD_DIM):
        carry, acc = block(h, i, jnp.zeros((tb, 1), F32), jnp.zeros((tb, HEAD_DIM), F32), True)

        def cond(state):
            j, carry, _ = state
            return jnp.logical_and(j >= 0, jnp.min(carry) < SKIP_THRESHOLD)

        def body(state):
            j, carry, acc = state
            carry, acc = block(h, j, carry, acc, False)
            return j - 1, carry, acc

        _, _, acc = lax.while_loop(cond, body, (i - 1, carry, acc))
        o_ref[:, h * HEAD_DIM:(h + 1) * HEAD_DIM] = acc.astype(BF16)


def _attention(q, k, v, *, batch, seq, tb):
    t, d = q.shape
    nq = seq // tb
    qspec = pl.BlockSpec((tb, LANES), lambda b, hp, i: (b * nq + i, hp))
    kvspec = pl.BlockSpec((seq, LANES), lambda b, hp, i: (b, hp))
    return pl.pallas_call(
        functools.partial(_attn_kernel, tb=tb),
        grid=(batch, d // LANES, nq),
        in_specs=[qspec, kvspec, kvspec],
        out_specs=qspec,
        out_shape=jax.ShapeDtypeStruct((t, d), BF16),
        compiler_params=_params(("parallel", "parallel", "arbitrary")),
        name="attention",
    )(q, k, v)


def _conv_kernel(prev_ref, cur_ref, w_ref, b_ref, g_ref, nb_ref, o_ref, win_ref, conv_ref, *, ts, d):
    i = pl.program_id(1)
    prev = prev_ref[...].astype(F32)
    win_ref[0:CONV_HALO, :] = jnp.where(i == 0, 0.0, prev)
    win_ref[CONV_HALO:, :] = cur_ref[...].astype(F32)
    base = CONV_HALO - (CONV_WIDTH - 1)
    for c in range(d // LANES):
        lanes = slice(c * LANES, (c + 1) * LANES)
        acc = jnp.zeros((ts, LANES), F32)
        for w in range(CONV_WIDTH):
            acc = acc + win_ref[base + w:base + w + ts, lanes] * w_ref[w:w + 1, lanes]
        conv_ref[:, lanes] = acc + b_ref[:, lanes]
    y = conv_ref[...]
    mu = jnp.mean(y, axis=-1, keepdims=True)
    yc = y - mu
    var = jnp.mean(yc * yc, axis=-1, keepdims=True)
    yn = yc * lax.rsqrt(var + EPS) * g_ref[...] + nb_ref[...]
    o_ref[...] = (yn * _sigmoid(yn)).astype(BF16)


def _conv_branch(u, dw_w, dw_b, norm_g, norm_b, *, batch, seq, ts):
    t, d = u.shape
    ns = seq // ts
    per = ts // CONV_HALO
    cur = pl.BlockSpec((ts, d), lambda b, i: (b * ns + i, 0))
    prev = pl.BlockSpec((CONV_HALO, d), lambda b, i: (jnp.maximum((b * ns + i) * per - 1, 0), 0))
    return pl.pallas_call(
        functools.partial(_conv_kernel, ts=ts, d=d),
        grid=(batch, ns),
        in_specs=[prev, cur, _resident(dw_w.shape), _resident((1, d)), _resident((1, d)), _resident((1, d))],
        out_specs=cur,
        out_shape=jax.ShapeDtypeStruct((t, d), BF16),
        scratch_shapes=[pltpu.VMEM((ts + CONV_HALO, d), F32), pltpu.VMEM((ts, d), F32)],
        compiler_params=_params(("parallel", "parallel")),
        name="conv_branch",
    )(u, u, dw_w, dw_b, norm_g, norm_b)


def _mix_kernel(x_ref, c_ref, o_ref, gate_ref, wc_ref, wa_ref, wo_ref, g2_ref, rw_hi_ref, rw_lo_ref, rb_ref,
                x1_ref, h2_ref, gw_ref, *, d):
    y_conv = jnp.dot(c_ref[...], wc_ref[...], preferred_element_type=F32)
    y_attn = jnp.dot(o_ref[...], wa_ref[...], preferred_element_type=F32)
    m = gate_ref[:, :d].astype(F32) * y_conv + gate_ref[:, d:].astype(F32) * y_attn
    x1 = x_ref[...] + jnp.dot(m.astype(BF16), wo_ref[...], preferred_element_type=F32)
    x1_ref[...] = x1
    ms = jnp.mean(x1 * x1, axis=-1, keepdims=True)
    h2 = x1 * lax.rsqrt(ms + EPS) * g2_ref[...]
    h2_hi = h2.astype(BF16)
    h2_ref[...] = h2_hi
    h2_lo = (h2 - h2_hi.astype(F32)).astype(BF16)
    logits = (jnp.dot(h2_hi, rw_hi_ref[...], preferred_element_type=F32)
              + jnp.dot(h2_lo, rw_hi_ref[...], preferred_element_type=F32)
              + jnp.dot(h2_hi, rw_lo_ref[...], preferred_element_type=F32)
              + rb_ref[...])
    lane = lax.broadcasted_iota(jnp.int32, logits.shape, 1)
    work = logits
    top = None
    gw = jnp.zeros_like(logits)
    for k in range(TOP_K):
        mx = jnp.max(work, axis=1, keepdims=True)
        idx = jnp.min(jnp.where(work == mx, lane, LANES), axis=1, keepdims=True)
        sel = lane == idx
        if k == 0:
            top = mx
        gw = jnp.where(sel, jnp.exp(logits - top), gw)
        work = jnp.where(sel, -jnp.inf, work)
    gw_ref[...] = gw / jnp.sum(gw, axis=1, keepdims=True)


def _mix(x2, c, o, gates, wc, wa, wo, g2, rw_hi, rw_lo, rb, *, tm):
    t, d = x2.shape
    row = lambda n: pl.BlockSpec((tm, n), lambda i: (i, 0))
    return pl.pallas_call(
        functools.partial(_mix_kernel, d=d),
        grid=(t // tm,),
        in_specs=[row(d), row(d), row(d), row(2 * d), _resident((d, d)), _resident((d, d)), _resident((d, d)),
                  _resident((1, d)), _resident((d, LANES)), _resident((d, LANES)), _resident((1, LANES))],
        out_specs=[row(d), row(d), row(LANES)],
        out_shape=[jax.ShapeDtypeStruct((t, d), F32), jax.ShapeDtypeStruct((t, d), BF16),
                   jax.ShapeDtypeStruct((t, LANES), F32)],
        compiler_params=_params(("parallel",)),
        name="mix_router",
    )(x2, c, o, gates, wc, wa, wo, g2, rw_hi, rw_lo, rb)


def _moe_kernel(x1_ref, h2_ref, gw_ref, w1_ref, b1_ref, w2_ref, b2_ref, fg_ref, o_ref, acc_ref, *, dff):
    e = pl.program_id(1)

    @pl.when(e == 0)
    def _():
        acc_ref[...] = jnp.zeros_like(acc_ref)

    gw = gw_ref[...]
    lane = lax.broadcasted_iota(jnp.int32, gw.shape, 1)
    wcol = jnp.sum(jnp.where(lane == e, gw, 0.0), axis=1, keepdims=True)

    hid = jnp.dot(h2_ref[...], w1_ref[0], preferred_element_type=F32) + b1_ref[0]
    glu = jnp.minimum(hid[:, :dff], SWIGLU_LIMIT)
    lin = jnp.clip(hid[:, dff:], -SWIGLU_LIMIT, SWIGLU_LIMIT)
    act = glu * _sigmoid(SWIGLU_ALPHA * glu) * (lin + 1.0)
    out = jnp.dot(act.astype(BF16), w2_ref[0], preferred_element_type=F32) + b2_ref[0]
    acc_ref[...] += wcol * out

    @pl.when(e == pl.num_programs(1) - 1)
    def _():
        y = x1_ref[...] + acc_ref[...]
        ms = jnp.mean(y * y, axis=-1, keepdims=True)
        o_ref[...] = y * lax.rsqrt(ms + EPS) * fg_ref[...]


def _moe(x1, h2, gw, w1, b1, w2, b2, fg, *, tm):
    t, d = x1.shape
    n_e, _, dff2 = w1.shape
    row = lambda n: pl.BlockSpec((tm, n), lambda i, e: (i, 0))
    per_e = lambda a, b: pl.BlockSpec((1, a, b), lambda i, e: (e, 0, 0))
    return pl.pallas_call(
        functools.partial(_moe_kernel, dff=dff2 // 2),
        grid=(t // tm, n_e),
        in_specs=[row(d), row(d), row(LANES), per_e(d, dff2), per_e(1, dff2), per_e(dff2 // 2, d), per_e(1, d),
                  _resident((1, d))],
        out_specs=row(d),
        out_shape=jax.ShapeDtypeStruct((t, d), F32),
        scratch_shapes=[pltpu.VMEM((tm, d), F32)],
        compiler_params=_params(("parallel", "arbitrary")),
        name="moe_experts",
    )(x1, h2, gw, w1, b1, w2, b2, fg)


def _tile(n, want):
    want = min(want, n)
    assert n % want == 0, (n, want)
    return want


def kernel(x, norm1_g, w_in, conv_dw_w, conv_dw_b, conv_norm_g, conv_norm_b, w_conv_out, w_attn_out, gate_b,
           w_out, norm2_g, router_w, router_b, expert_w1, expert_b1, expert_w2, expert_b2, final_norm_g):
    batch, seq, d = x.shape
    depth = w_in.shape[0]
    t = batch * seq
    assert d % LANES == 0 and N_EXPERTS <= LANES
    row2 = lambda a: a.reshape(1, -1)

    tb = _tile(seq, 256)
    x2 = x.reshape(t, d)
    for l in range(depth):
        u, q, k, v, gates = _in_proj(x2, row2(norm1_g[l]), w_in[l].astype(BF16), row2(gate_b[l]),
                                     tm=_tile(t, 512))
        o = _attention(q, k, v, batch=batch, seq=seq, tb=tb)
        c = _conv_branch(u, conv_dw_w[l], row2(conv_dw_b[l]), row2(conv_norm_g[l]), row2(conv_norm_b[l]),
                         batch=batch, seq=seq, ts=_tile(seq, 256))
        rw = jnp.pad(router_w[l], ((0, 0), (0, LANES - N_EXPERTS)))
        rw_hi = rw.astype(BF16)
        rw_lo = (rw - rw_hi.astype(F32)).astype(BF16)
        rb = jnp.pad(router_b[l].astype(F32), (0, LANES - N_EXPERTS), constant_values=-1e30).reshape(1, LANES)
        x1, h2, gw = _mix(x2, c, o, gates, w_conv_out[l].astype(BF16), w_attn_out[l].astype(BF16),
                          w_out[l].astype(BF16), row2(norm2_g[l]), rw_hi, rw_lo, rb, tm=_tile(t, 512))
        last = l == depth - 1
        fg = row2(final_norm_g) if last else jnp.ones((1, d), F32)
        assert last, "the final rms_norm is fused into the expert kernel of the last layer"
        x2 = _moe(x1, h2, gw, expert_w1[l].astype(BF16), expert_b1[l][:, None, :], expert_w2[l].astype(BF16),
                  expert_b2[l][:, None, :], fg, tm=_tile(t, 512))
    return x2.reshape(batch, seq, d)
```

```python
import functools

import jax
import jax.numpy as jnp
from jax import lax
from jax.experimental import pallas as pl
from jax.experimental.pallas import tpu as pltpu

F32 = jnp.float32
BF16 = jnp.bfloat16

HEAD_DIM = 64
CONV_WIDTH = 31
N_EXPERTS = 32
TOP_K = 4
SWIGLU_LIMIT = 7.0
SWIGLU_ALPHA = 1.702
EPS = 1e-6

LANES = 128
SLAB = 8
ISSUE_GROUP = 8
CONV_HALO = 32
VMEM_LIMIT = 56 * 1024 * 1024

SKIP_THRESHOLD = 104.0


def _params(sem):
    return pltpu.CompilerParams(dimension_semantics=sem, vmem_limit_bytes=VMEM_LIMIT)


def _resident(shape):
    return pl.BlockSpec(shape, lambda *_: (0,) * len(shape), pipeline_mode=pl.Buffered(1))


def _sigmoid(x):
    return 1.0 / (1.0 + jnp.exp(-x))


def _inproj_kernel(x_ref, g_ref, w_ref, gb_ref, u_ref, q_ref, k_ref, v_ref, gate_ref, *, d):
    x = x_ref[...]
    ms = jnp.mean(x * x, axis=-1, keepdims=True)
    h = (x * lax.rsqrt(ms + EPS) * g_ref[...]).astype(BF16)

    def proj(c):
        return jnp.dot(h, w_ref[:, c * d:(c + 1) * d], preferred_element_type=F32)

    u_ref[...] = (proj(0) * _sigmoid(proj(1))).astype(BF16)
    q_ref[...] = (proj(2) * (HEAD_DIM ** -0.5)).astype(BF16)
    k_ref[...] = proj(3).astype(BF16)
    v_ref[...] = proj(4).astype(BF16)
    gate_ref[:, :d] = _sigmoid(proj(5) + gb_ref[:, :d]).astype(BF16)
    gate_ref[:, d:] = _sigmoid(proj(6) + gb_ref[:, d:]).astype(BF16)


def _in_proj(x2, norm_g, w_in, gate_b, *, tm):
    t, d = x2.shape
    row = lambda n: pl.BlockSpec((tm, n), lambda i: (i, 0))
    out = lambda n: jax.ShapeDtypeStruct((t, n), BF16)
    return pl.pallas_call(
        functools.partial(_inproj_kernel, d=d),
        grid=(t // tm,),
        in_specs=[row(d), _resident((1, d)), _resident(w_in.shape), _resident((1, 2 * d))],
        out_specs=[row(d), row(d), row(d), row(d), row(2 * d)],
        out_shape=[out(d), out(d), out(d), out(d), out(2 * d)],
        compiler_params=_params(("parallel",)),
        name="in_proj",
    )(x2, norm_g, w_in, gate_b)


def _attn_kernel(q_ref, k_ref, v_ref, o_ref, *, tb):
    i = pl.program_id(2)
    r_iota = lax.broadcasted_iota(jnp.int32, (tb, tb), 0)
    c_iota = lax.broadcasted_iota(jnp.int32, (tb, tb), 1)
    upper = jnp.where(r_iota > c_iota, 1.0, 0.0).astype(BF16)
    causal = c_iota < r_iota

    def block(h, j, carry, acc, diagonal):
        lanes = slice(h * HEAD_DIM, (h + 1) * HEAD_DIM)
        start = pl.multiple_of(j * tb, tb)
        qh = q_ref[:, lanes]
        kh = k_ref[pl.ds(start, tb), lanes]
        vh = v_ref[pl.ds(start, tb), lanes]
        z = lax.dot_general(qh, kh, (((1,), (1,)), ((), ())), preferred_element_type=F32)
        l = jnp.log1p(jnp.exp(-jnp.abs(z)))
        sp = jnp.maximum(z, 0.0) + l
        logsig = jnp.minimum(z, 0.0) - l
        if diagonal:
            sp = jnp.where(causal, sp, 0.0)
        hi = sp.astype(BF16)
        lo = (sp - hi.astype(F32)).astype(BF16)
        later = (jnp.dot(hi, upper, preferred_element_type=F32)
                 + jnp.dot(lo, upper, preferred_element_type=F32) + carry)
        a = jnp.exp(logsig - later)
        if diagonal:
            a = jnp.where(causal, a, 0.0)
        acc = acc + jnp.dot(a.astype(BF16), vh, preferred_element_type=F32)
        carry = carry + jnp.sum(sp, axis=1, keepdims=True)
        return carry, acc

    for h in range(LANES // HEAD_DIM):
        carry, acc = block(h, i, jnp.zeros((tb, 1), F32), jnp.zeros((tb, HEAD_DIM), F32), True)

        def cond(state):
            j, carry, _ = state
            return jnp.logical_and(j >= 0, jnp.min(carry) < SKIP_THRESHOLD)

        def body(state):
            j, carry, acc = state
            carry, acc = block(h, j, carry, acc, False)
            return j - 1, carry, acc

        _, _, acc = lax.while_loop(cond, body, (i - 1, carry, acc))
        o_ref[:, h * HEAD_DIM:(h + 1) * HEAD_DIM] = acc.astype(BF16)


def _attention(q, k, v, *, batch, seq, tb):
    t, d = q.shape
    nq = seq // tb
    qspec = pl.BlockSpec((tb, LANES), lambda b, hp, i: (b * nq + i, hp))
    kvspec = pl.BlockSpec((seq, LANES), lambda b, hp, i: (b, hp))
    return pl.pallas_call(
        functools.partial(_attn_kernel, tb=tb),
        grid=(batch, d // LANES, nq),
        in_specs=[qspec, kvspec, kvspec],
        out_specs=qspec,
        out_shape=jax.ShapeDtypeStruct((t, d), BF16),
        compiler_params=_params(("parallel", "parallel", "arbitrary")),
        name="attention",
    )(q, k, v)


def _conv_kernel(prev_ref, cur_ref, w_ref, b_ref, g_ref, nb_ref, o_ref, win_ref, conv_ref, *, ts, d):
    i = pl.program_id(1)
    prev = prev_ref[...].astype(F32)
    win_ref[0:CONV_HALO, :] = jnp.where(i == 0, 0.0, prev)
    win_ref[CONV_HALO:, :] = cur_ref[...].astype(F32)
    base = CONV_HALO - (CONV_WIDTH - 1)
    for c in range(d // LANES):
        lanes = slice(c * LANES, (c + 1) * LANES)
        acc = jnp.zeros((ts, LANES), F32)
        for w in range(CONV_WIDTH):
            acc = acc + win_ref[base + w:base + w + ts, lanes] * w_ref[w:w + 1, lanes]
        conv_ref[:, lanes] = acc + b_ref[:, lanes]
    y = conv_ref[...]
    mu = jnp.mean(y, axis=-1, keepdims=True)
    yc = y - mu
    var = jnp.mean(yc * yc, axis=-1, keepdims=True)
    yn = yc * lax.rsqrt(var + EPS) * g_ref[...] + nb_ref[...]
    o_ref[...] = (yn * _sigmoid(yn)).astype(BF16)


def _conv_branch(u, dw_w, dw_b, norm_g, norm_b, *, batch, seq, ts):
    t, d = u.shape
    ns = seq // ts
    per = ts // CONV_HALO
    cur = pl.BlockSpec((ts, d), lambda b, i: (b * ns + i, 0))
    prev = pl.BlockSpec((CONV_HALO, d), lambda b, i: (jnp.maximum((b * ns + i) * per - 1, 0), 0))
    return pl.pallas_call(
        functools.partial(_conv_kernel, ts=ts, d=d),
        grid=(batch, ns),
        in_specs=[prev, cur, _resident(dw_w.shape), _resident((1, d)), _resident((1, d)), _resident((1, d))],
        out_specs=cur,
        out_shape=jax.ShapeDtypeStruct((t, d), BF16),
        scratch_shapes=[pltpu.VMEM((ts + CONV_HALO, d), F32), pltpu.VMEM((ts, d), F32)],
        compiler_params=_params(("parallel", "parallel")),
        name="conv_branch",
    )(u, u, dw_w, dw_b, norm_g, norm_b)


def _to_slabs(ref, val, rows):
    for c in range(SLAB):
        ref[pl.ds(c, rows, stride=SLAB), :] = val[:, c * LANES:(c + 1) * LANES]


def _from_slabs(ref, rows, lead=()):
    return jnp.concatenate([ref[lead + (pl.ds(c, rows, stride=SLAB), slice(None))] for c in range(SLAB)], axis=1)


def _mix_kernel(x_ref, c_ref, o_ref, gate_ref, wc_ref, wa_ref, wo_ref, g2_ref, rw_hi_ref, rw_lo_ref, rb_ref,
                x1_ref, h2s_ref, topi_ref, topw_ref, cnt_ref, *, d, tm):
    y_conv = jnp.dot(c_ref[...], wc_ref[...], preferred_element_type=F32)
    y_attn = jnp.dot(o_ref[...], wa_ref[...], preferred_element_type=F32)
    m = gate_ref[:, :d].astype(F32) * y_conv + gate_ref[:, d:].astype(F32) * y_attn
    x1 = x_ref[...] + jnp.dot(m.astype(BF16), wo_ref[...], preferred_element_type=F32)
    x1_ref[...] = x1
    ms = jnp.mean(x1 * x1, axis=-1, keepdims=True)
    h2 = x1 * lax.rsqrt(ms + EPS) * g2_ref[...]
    h2_hi = h2.astype(BF16)
    _to_slabs(h2s_ref, h2_hi.astype(F32), tm)
    h2_lo = (h2 - h2_hi.astype(F32)).astype(BF16)
    logits = (jnp.dot(h2_hi, rw_hi_ref[...], preferred_element_type=F32)
              + jnp.dot(h2_lo, rw_hi_ref[...], preferred_element_type=F32)
              + jnp.dot(h2_hi, rw_lo_ref[...], preferred_element_type=F32)
              + rb_ref[...])
    lane = lax.broadcasted_iota(jnp.int32, logits.shape, 1)
    work = logits
    top = None
    topi = jnp.zeros(logits.shape, jnp.int32)
    topw = jnp.zeros_like(logits)
    chosen = jnp.zeros_like(logits)
    for k in range(TOP_K):
        mx = jnp.max(work, axis=1, keepdims=True)
        idx = jnp.min(jnp.where(work == mx, lane, LANES), axis=1, keepdims=True)
        sel = lane == idx
        if k == 0:
            top = mx
        topi = jnp.where(lane == k, idx, topi)
        topw = jnp.where(lane == k, jnp.exp(mx - top), topw)
        chosen = jnp.where(sel, 1.0, chosen)
        work = jnp.where(sel, -jnp.inf, work)
    topi_ref[...] = topi
    topw_ref[...] = topw / jnp.sum(topw, axis=1, keepdims=True)
    cnt_ref[0] = jnp.sum(chosen, axis=0, keepdims=True)


def _mix(x2, c, o, gates, wc, wa, wo, g2, rw_hi, rw_lo, rb, *, tm):
    t, d = x2.shape
    row = lambda n: pl.BlockSpec((tm, n), lambda i: (i, 0))
    return pl.pallas_call(
        functools.partial(_mix_kernel, d=d, tm=tm),
        grid=(t // tm,),
        in_specs=[row(d), row(d), row(d), row(2 * d), _resident((d, d)), _resident((d, d)), _resident((d, d)),
                  _resident((1, d)), _resident((d, LANES)), _resident((d, LANES)), _resident((1, LANES))],
        out_specs=[row(d), pl.BlockSpec((tm * SLAB, LANES), lambda i: (i, 0)), row(LANES), row(LANES),
                   pl.BlockSpec((1, 1, LANES), lambda i: (i, 0, 0))],
        out_shape=[jax.ShapeDtypeStruct((t, d), F32), jax.ShapeDtypeStruct((t * SLAB, LANES), F32),
                   jax.ShapeDtypeStruct((t, LANES), jnp.int32), jax.ShapeDtypeStruct((t, LANES), F32),
                   jax.ShapeDtypeStruct((t // tm, 1, LANES), F32)],
        compiler_params=_params(("parallel",)),
        name="mix_router",
    )(x2, c, o, gates, wc, wa, wo, g2, rw_hi, rw_lo, rb)


def _plan_kernel(topi_ref, base_ref, dest_ref, *, tt):
    idx = topi_ref[...]
    lane = lax.broadcasted_iota(jnp.int32, idx.shape, 1)
    onehots = [lane == idx[:, k:k + 1] for k in range(TOP_K)]
    chosen = jnp.zeros(idx.shape, F32)
    for oh in onehots:
        chosen = jnp.where(oh, 1.0, chosen)
    r_iota = lax.broadcasted_iota(jnp.int32, (tt, tt), 0)
    c_iota = lax.broadcasted_iota(jnp.int32, (tt, tt), 1)
    earlier = jnp.where(c_iota < r_iota, 1.0, 0.0).astype(BF16)
    rank = jnp.dot(earlier, chosen.astype(BF16), preferred_element_type=F32)
    pos = rank + base_ref[0]
    dest = jnp.zeros(idx.shape, F32)
    for k, oh in enumerate(onehots):
        dk = jnp.sum(jnp.where(oh, pos, 0.0), axis=1, keepdims=True)
        dest = jnp.where(lane == k, dk, dest)
    dest_ref[...] = dest.T[0:SLAB, :].astype(jnp.int32)


def _plan(topi, base, *, tt):
    t = topi.shape[0]
    nt = t // tt
    return pl.pallas_call(
        functools.partial(_plan_kernel, tt=tt),
        grid=(nt,),
        in_specs=[pl.BlockSpec((tt, LANES), lambda i: (i, 0)), pl.BlockSpec((1, 1, LANES), lambda i: (i, 0, 0))],
        out_specs=pl.BlockSpec((SLAB, tt), lambda i: (i, 0)),
        out_shape=jax.ShapeDtypeStruct((nt * SLAB, tt), jnp.int32),
        compiler_params=_params(("parallel",)),
        name="moe_plan",
    )(topi, base)


def _row_copy(src_ref, src_row, dst_ref, dst_row, sem):
    return pltpu.make_async_copy(src_ref.at[pl.ds(pl.multiple_of(src_row * SLAB, SLAB), SLAB)],
                                 dst_ref.at[pl.ds(pl.multiple_of(dst_row * SLAB, SLAB), SLAB)], sem)


def _dispatch_kernel(pad_start_ref, pad_cnt_ref, dest_hbm, h2s_hbm, xs_hbm, dest_smem, zero_ref, sem, isem,
                     *, tt, n_e):
    i = pl.program_id(0)
    load = pltpu.make_async_copy(dest_hbm.at[pl.ds(pl.multiple_of(i * SLAB, SLAB), SLAB)], dest_smem, isem)
    load.start()

    @pl.when(i == 0)
    def _():
        zero_ref[...] = jnp.zeros_like(zero_ref)
        for e in range(n_e):
            def fill(j, carry):
                cp = _row_copy(zero_ref, 0, xs_hbm, pad_start_ref[e] + j, sem)
                cp.start()
                cp.wait()
                return carry
            lax.fori_loop(0, pad_cnt_ref[e], fill, 0)

    load.wait()

    def issue(g, carry):
        toks = [g * ISSUE_GROUP + j for j in range(ISSUE_GROUP)]
        rows = [[dest_smem[k, tok] for k in range(TOP_K)] for tok in toks]
        for tok, row in zip(toks, rows):
            for k in range(TOP_K):
                _row_copy(h2s_hbm, i * tt + tok, xs_hbm, row[k], sem).start()
        return carry
    lax.fori_loop(0, tt // ISSUE_GROUP, issue, 0)

    def drain(tok, carry):
        for k in range(TOP_K):
            _row_copy(h2s_hbm, 0, xs_hbm, 0, sem).wait()
        return carry
    lax.fori_loop(0, tt, drain, 0, unroll=8)


def _dispatch(pad_start, pad_cnt, dest, h2s, *, rows, tt):
    nt = dest.shape[0] // SLAB
    n_e = pad_start.shape[0]
    anyspec = pl.BlockSpec(memory_space=pl.ANY)
    return pl.pallas_call(
        functools.partial(_dispatch_kernel, tt=tt, n_e=n_e),
        grid_spec=pltpu.PrefetchScalarGridSpec(
            num_scalar_prefetch=2, grid=(nt,), in_specs=[anyspec, anyspec], out_specs=anyspec,
            scratch_shapes=[pltpu.SMEM((SLAB, tt), jnp.int32), pltpu.VMEM((SLAB, LANES), F32),
                            pltpu.SemaphoreType.DMA, pltpu.SemaphoreType.DMA]),
        out_shape=jax.ShapeDtypeStruct((rows * SLAB, LANES), F32),
        compiler_params=_params(("arbitrary",)),
        name="moe_dispatch",
    )(pad_start, pad_cnt, dest, h2s)


def _ffn_kernel(te_ref, nu_ref, xs_ref, w1_ref, b1_ref, w2_ref, b2_ref, o_ref, *, tr, dff):
    @pl.when(pl.program_id(0) < nu_ref[0])
    def _():
        xs = _from_slabs(xs_ref, tr).astype(BF16)
        hid = jnp.dot(xs, w1_ref[0], preferred_element_type=F32) + b1_ref[0]
        glu = jnp.minimum(hid[:, :dff], SWIGLU_LIMIT)
        lin = jnp.clip(hid[:, dff:], -SWIGLU_LIMIT, SWIGLU_LIMIT)
        act = glu * _sigmoid(SWIGLU_ALPHA * glu) * (lin + 1.0)
        out = jnp.dot(act.astype(BF16), w2_ref[0], preferred_element_type=F32) + b2_ref[0]
        _to_slabs(o_ref, out, tr)


def _ffn(tile_expert, n_used, xs, w1, b1, w2, b2, *, tr):
    n_e, d, dff2 = w1.shape
    n_tiles = tile_expert.shape[0]
    rows = lambda r, te, nu: (jnp.minimum(r, nu[0] - 1), 0)
    per_e = lambda a, b: pl.BlockSpec((1, a, b), lambda r, te, nu: (te[r], 0, 0))
    return pl.pallas_call(
        functools.partial(_ffn_kernel, tr=tr, dff=dff2 // 2),
        grid_spec=pltpu.PrefetchScalarGridSpec(
            num_scalar_prefetch=2, grid=(n_tiles,),
            in_specs=[pl.BlockSpec((tr * SLAB, LANES), rows), per_e(d, dff2), per_e(1, dff2), per_e(dff2 // 2, d),
                      per_e(1, d)],
            out_specs=pl.BlockSpec((tr * SLAB, LANES), rows)),
        out_shape=jax.ShapeDtypeStruct(xs.shape, F32),
        compiler_params=_params(("arbitrary",)),
        name="moe_experts",
    )(tile_expert, n_used, xs, w1, b1, w2, b2)


def _combine_kernel(dest_hbm, ys_hbm, x1_ref, topw_ref, fg_ref, o_ref, dest_smem, buf_ref, sem, isem, *, tt):
    i = pl.program_id(0)
    load = pltpu.make_async_copy(dest_hbm.at[pl.ds(pl.multiple_of(i * SLAB, SLAB), SLAB)], dest_smem, isem)
    load.start()
    load.wait()

    def issue(g, carry):
        toks = [g * ISSUE_GROUP + j for j in range(ISSUE_GROUP)]
        rows = [[dest_smem[k, tok] for k in range(TOP_K)] for tok in toks]
        for tok, row in zip(toks, rows):
            for k in range(TOP_K):
                _row_copy(ys_hbm, row[k], buf_ref.at[k], tok, sem).start()
        return carry
    lax.fori_loop(0, tt // ISSUE_GROUP, issue, 0)

    def drain(tok, carry):
        for k in range(TOP_K):
            _row_copy(ys_hbm, 0, buf_ref.at[k], 0, sem).wait()
        return carry
    lax.fori_loop(0, tt, drain, 0, unroll=8)

    y = x1_ref[...]
    w = topw_ref[...]
    for k in range(TOP_K):
        y = y + w[:, k:k + 1] * _from_slabs(buf_ref, tt, lead=(k,))
    ms = jnp.mean(y * y, axis=-1, keepdims=True)
    o_ref[...] = y * lax.rsqrt(ms + EPS) * fg_ref[...]


def _combine(dest, ys, x1, topw, fg, *, tt):
    t, d = x1.shape
    row = lambda n: pl.BlockSpec((tt, n), lambda i: (i, 0))
    anyspec = pl.BlockSpec(memory_space=pl.ANY)
    return pl.pallas_call(
        functools.partial(_combine_kernel, tt=tt),
        grid=(t // tt,),
        in_specs=[anyspec, anyspec, row(d), row(LANES), _resident((1, d))],
        out_specs=row(d),
        out_shape=jax.ShapeDtypeStruct((t, d), F32),
        scratch_shapes=[pltpu.SMEM((SLAB, tt), jnp.int32), pltpu.VMEM((TOP_K, tt * SLAB, LANES), F32),
                        pltpu.SemaphoreType.DMA, pltpu.SemaphoreType.DMA],
        compiler_params=_params(("arbitrary",)),
        name="moe_combine",
    )(dest, ys, x1, topw, fg)


def _moe(x1, h2s, topi, topw, counts, w1, b1, w2, b2, fg, *, tt, tr):
    t = x1.shape[0]
    n_e = w1.shape[0]
    cnt = counts[:, 0, :n_e].astype(jnp.int32)
    total = jnp.sum(cnt, axis=0)
    padded = (total + tr - 1) // tr * tr
    ends = jnp.cumsum(padded)
    starts = ends - padded
    base = starts[None, :] + jnp.cumsum(cnt, axis=0) - cnt
    base = jnp.pad(base.astype(F32), ((0, 0), (0, LANES - n_e)))[:, None, :]
    n_tiles = (t * TOP_K) // tr + n_e
    n_used = (ends[-1] // tr).astype(jnp.int32)
    r = jnp.arange(n_tiles, dtype=jnp.int32)
    tile_expert = jnp.searchsorted(ends // tr, jnp.minimum(r, n_used - 1), side="right").astype(jnp.int32)

    dest = _plan(topi, base, tt=tt)
    xs = _dispatch(starts + total, padded - total, dest, h2s, rows=n_tiles * tr, tt=tt)
    ys = _ffn(tile_expert, n_used.reshape(1), xs, w1, b1, w2, b2, tr=tr)
    return _combine(dest, ys, x1, topw, fg, tt=tt)


def _tile(n, want):
    want = min(want, n)
    assert n % want == 0, (n, want)
    return want


def kernel(x, norm1_g, w_in, conv_dw_w, conv_dw_b, conv_norm_g, conv_norm_b, w_conv_out, w_attn_out, gate_b,
           w_out, norm2_g, router_w, router_b, expert_w1, expert_b1, expert_w2, expert_b2, final_norm_g):
    batch, seq, d = x.shape
    depth = w_in.shape[0]
    t = batch * seq
    assert d == SLAB * LANES and N_EXPERTS <= LANES
    assert depth == 1, "the final rms_norm is fused into the combine kernel of the only layer"
    row2 = lambda a: a.reshape(1, -1)
    l = 0

    tb = _tile(seq, 256)
    tt = _tile(t, 512)
    x2 = x.reshape(t, d)
    u, q, k, v, gates = _in_proj(x2, row2(norm1_g[l]), w_in[l].astype(BF16), row2(gate_b[l]), tm=_tile(t, 512))
    o = _attention(q, k, v, batch=batch, seq=seq, tb=tb)
    c = _conv_branch(u, conv_dw_w[l], row2(conv_dw_b[l]), row2(conv_norm_g[l]), row2(conv_norm_b[l]),
                     batch=batch, seq=seq, ts=_tile(seq, 256))
    rw = jnp.pad(router_w[l], ((0, 0), (0, LANES - N_EXPERTS)))
    rw_hi = rw.astype(BF16)
    rw_lo = (rw - rw_hi.astype(F32)).astype(BF16)
    rb = jnp.pad(router_b[l].astype(F32), (0, LANES - N_EXPERTS), constant_values=-1e30).reshape(1, LANES)
    x1, h2s, topi, topw, counts = _mix(x2, c, o, gates, w_conv_out[l].astype(BF16), w_attn_out[l].astype(BF16),
                                       w_out[l].astype(BF16), row2(norm2_g[l]), rw_hi, rw_lo, rb, tm=tt)
    out = _moe(x1, h2s, topi, topw, counts, expert_w1[l].astype(BF16), expert_b1[l][:, None, :],
               expert_w2[l].astype(BF16), expert_b2[l][:, None, :], row2(final_norm_g), tt=tt,
               tr=_tile(t * TOP_K, 512))
    return out.reshape(batch, seq, d)
```

```python
import functools

import jax
import jax.numpy as jnp
from jax import lax
from jax.experimental import pallas as pl
from jax.experimental.pallas import tpu as pltpu

F32 = jnp.float32
BF16 = jnp.bfloat16

HEAD_DIM = 64
CONV_WIDTH = 31
N_EXPERTS = 32
TOP_K = 4
SWIGLU_LIMIT = 7.0
SWIGLU_ALPHA = 1.702
EPS = 1e-6

LANES = 128
SLAB = 8
ATTN_HEADS_PER_STEP = 4
ISSUE_GROUP = 8
CONV_HALO = 32
CONV_ROWS = 128
VMEM_LIMIT = 56 * 1024 * 1024

LOG2_E = 1.4426950408889634
SKIP_THRESHOLD = 150.0 * (1.0 + 2.0 ** -7)
SIGN_BIT = 0x80000000


def _params(sem):
    return pltpu.CompilerParams(dimension_semantics=sem, vmem_limit_bytes=VMEM_LIMIT)


def _resident(shape):
    return pl.BlockSpec(shape, lambda *_: (0,) * len(shape), pipeline_mode=pl.Buffered(1))


def _sigmoid(x):
    return 1.0 / (1.0 + jnp.exp(-x))


def _inproj_kernel(x_ref, g_ref, w_ref, gb_ref, u_ref, q_ref, k_ref, v_ref, gate_ref, *, d):
    x = x_ref[...]
    ms = jnp.mean(x * x, axis=-1, keepdims=True)
    h = (x * lax.rsqrt(ms + EPS) * g_ref[...]).astype(BF16)

    def proj(c):
        return jnp.dot(h, w_ref[:, c * d:(c + 1) * d], preferred_element_type=F32)

    u_ref[...] = (proj(0) * _sigmoid(proj(1))).astype(BF16)
    q_ref[...] = (proj(2) * (HEAD_DIM ** -0.5 * LOG2_E)).astype(BF16)
    k_ref[...] = proj(3).astype(BF16)
    v_ref[...] = proj(4).astype(BF16)
    gate_ref[:, :d] = _sigmoid(proj(5) + gb_ref[:, :d]).astype(BF16)
    gate_ref[:, d:] = _sigmoid(proj(6) + gb_ref[:, d:]).astype(BF16)


def _in_proj(x2, norm_g, w_in, gate_b, *, tm):
    t, d = x2.shape
    row = lambda n: pl.BlockSpec((tm, n), lambda i: (i, 0))
    out = lambda n: jax.ShapeDtypeStruct((t, n), BF16)
    return pl.pallas_call(
        functools.partial(_inproj_kernel, d=d),
        grid=(t // tm,),
        in_specs=[row(d), _resident((1, d)), _resident(w_in.shape), _resident((1, 2 * d))],
        out_specs=[row(d), row(d), row(d), row(d), row(2 * d)],
        out_shape=[out(d), out(d), out(d), out(d), out(2 * d)],
        compiler_params=_params(("parallel",)),
        name="in_proj",
    )(x2, norm_g, w_in, gate_b)


def _attn_kernel(q_ref, k_ref, v_ref, o_ref, *, tb, heads):
    i = pl.program_id(2)
    r_iota = lax.broadcasted_iota(jnp.int32, (tb, tb), 0)
    c_iota = lax.broadcasted_iota(jnp.int32, (tb, tb), 1)
    upper = jnp.where(r_iota > c_iota, 1.0, 0.0).astype(BF16)
    causal = c_iota < r_iota

    first_head = lax.broadcasted_iota(jnp.int32, (tb, LANES), 1) < HEAD_DIM
    zeros = jnp.zeros((tb, LANES), BF16)

    def split_heads(x2):
        return jnp.concatenate([jnp.where(first_head, x2, zeros), jnp.where(first_head, zeros, x2)], axis=0)

    def pair_block(p, j, carry, acc, diagonal):
        lanes = slice(p * LANES, (p + 1) * LANES)
        start = pl.multiple_of(j * tb, tb)
        z = lax.dot_general(q_ref[:, lanes], split_heads(k_ref[pl.ds(start, tb), lanes]),
                            (((1,), (1,)), ((), ())), preferred_element_type=F32)
        neg_abs = pltpu.bitcast(pltpu.bitcast(z, jnp.uint32) | jnp.uint32(SIGN_BIT), F32)
        l = jnp.log2(1.0 + jnp.exp2(neg_abs))
        sp = jnp.maximum(z, 0.0) + l
        logsig = jnp.minimum(z, 0.0) - l
        if diagonal:
            sp = jnp.where(causal2, sp, 0.0)
        spb = sp.astype(BF16)
        prefix = jnp.dot(jnp.concatenate([spb[:, :tb], spb[:, tb:]], axis=0), upper,
                         preferred_element_type=F32)
        mass = (prefix[0:tb, 0:1] + sp[:, 0:1], prefix[tb:, 0:1] + sp[:, tb:tb + 1])
        later = jnp.concatenate([prefix[0:tb] + carry[0], prefix[tb:] + carry[1]], axis=1)
        a = jnp.exp2(logsig - later)
        if diagonal:
            a = jnp.where(causal2, a, 0.0)
        acc = acc + jnp.dot(a.astype(BF16), split_heads(v_ref[pl.ds(start, tb), lanes]),
                            preferred_element_type=F32)
        return (carry[0] + mass[0], carry[1] + mass[1]), acc

    pairs = heads * HEAD_DIM // LANES
    causal2 = jnp.concatenate([causal, causal], axis=1)

    def blocks(j, carries, accs, diagonal):
        new = [pair_block(p, j, carries[p], accs[p], diagonal) for p in range(pairs)]
        return tuple(c for c, _ in new), tuple(a for _, a in new)

    zero_carry = (jnp.zeros((tb, 1), F32), jnp.zeros((tb, 1), F32))
    carries, accs = blocks(i, (zero_carry,) * pairs, (jnp.zeros((tb, LANES), F32),) * pairs, True)

    def cond(state):
        j, carries, _ = state
        least = functools.reduce(jnp.minimum, [c for pair in carries for c in pair])
        return jnp.logical_and(j >= 0, jnp.min(least) < SKIP_THRESHOLD)

    def body(state):
        j, carries, accs = state
        carries, accs = blocks(j, carries, accs, False)
        return j - 1, carries, accs

    _, _, accs = lax.while_loop(cond, body, (i - 1, carries, accs))
    for p in range(pairs):
        o_ref[:, p * LANES:(p + 1) * LANES] = accs[p].astype(BF16)


def _attention(q, k, v, *, batch, seq, tb, heads):
    t, d = q.shape
    nq = seq // tb
    width = heads * HEAD_DIM
    qspec = pl.BlockSpec((tb, width), lambda b, hp, i: (b * nq + i, hp))
    kvspec = pl.BlockSpec((seq, width), lambda b, hp, i: (b, hp))
    return pl.pallas_call(
        functools.partial(_attn_kernel, tb=tb, heads=heads),
        grid=(batch, d // width, nq),
        in_specs=[qspec, kvspec, kvspec],
        out_specs=qspec,
        out_shape=jax.ShapeDtypeStruct((t, d), BF16),
        compiler_params=_params(("parallel", "parallel", "arbitrary")),
        name="attention",
    )(q, k, v)


def _conv_kernel(prev_ref, cur_ref, w_ref, b_ref, g_ref, nb_ref, o_ref, win_ref, conv_ref, *, ts, d):
    i = pl.program_id(1)
    prev = prev_ref[...].astype(F32)
    win_ref[0:CONV_HALO, :] = jnp.where(i == 0, 0.0, prev)
    win_ref[CONV_HALO:, :] = cur_ref[...].astype(F32)
    base = CONV_HALO - (CONV_WIDTH - 1)
    rows = CONV_ROWS
    span = rows + CONV_HALO
    for c in range(d // LANES):
        lanes = slice(c * LANES, (c + 1) * LANES)
        for r0 in range(0, ts, rows):
            window = win_ref[r0:r0 + span, lanes]
            acc = jnp.zeros((rows, LANES), F32)
            for shift in range(SLAB):
                shifted = pltpu.roll(window, (span - shift) % span, axis=0) if shift else window
                for w in range(CONV_WIDTH):
                    if (base + w) % SLAB == shift:
                        lead = (base + w) // SLAB * SLAB
                        acc = acc + shifted[lead:lead + rows] * w_ref[w:w + 1, lanes]
            conv_ref[r0:r0 + rows, lanes] = acc + b_ref[:, lanes]
    y = conv_ref[...]
    mu = jnp.mean(y, axis=-1, keepdims=True)
    yc = y - mu
    var = jnp.mean(yc * yc, axis=-1, keepdims=True)
    yn = yc * lax.rsqrt(var + EPS) * g_ref[...] + nb_ref[...]
    o_ref[...] = (yn * _sigmoid(yn)).astype(BF16)


def _conv_branch(u, dw_w, dw_b, norm_g, norm_b, *, batch, seq, ts):
    t, d = u.shape
    ns = seq // ts
    per = ts // CONV_HALO
    cur = pl.BlockSpec((ts, d), lambda b, i: (b * ns + i, 0))
    prev = pl.BlockSpec((CONV_HALO, d), lambda b, i: (jnp.maximum((b * ns + i) * per - 1, 0), 0))
    return pl.pallas_call(
        functools.partial(_conv_kernel, ts=ts, d=d),
        grid=(batch, ns),
        in_specs=[prev, cur, _resident(dw_w.shape), _resident((1, d)), _resident((1, d)), _resident((1, d))],
        out_specs=cur,
        out_shape=jax.ShapeDtypeStruct((t, d), BF16),
        scratch_shapes=[pltpu.VMEM((ts + CONV_HALO, d), F32), pltpu.VMEM((ts, d), F32)],
        compiler_params=_params(("parallel", "parallel")),
        name="conv_branch",
    )(u, u, dw_w, dw_b, norm_g, norm_b)


def _to_slabs(ref, val, rows):
    for c in range(SLAB):
        ref[pl.ds(c, rows, stride=SLAB), :] = val[:, c * LANES:(c + 1) * LANES]


def _from_slabs(ref, rows, lead=()):
    return jnp.concatenate([ref[lead + (pl.ds(c, rows, stride=SLAB), slice(None))] for c in range(SLAB)], axis=1)


def _mix_kernel(x_ref, c_ref, o_ref, gate_ref, wc_ref, wa_ref, wo_ref, g2_ref, rw_hi_ref, rw_lo_ref, rb_ref,
                x1_ref, h2s_ref, topi_ref, topw_ref, cnt_ref, *, d, tm):
    y_conv = jnp.dot(c_ref[...], wc_ref[...], preferred_element_type=F32)
    y_attn = jnp.dot(o_ref[...], wa_ref[...], preferred_element_type=F32)
    m = gate_ref[:, :d].astype(F32) * y_conv + gate_ref[:, d:].astype(F32) * y_attn
    x1 = x_ref[...] + jnp.dot(m.astype(BF16), wo_ref[...], preferred_element_type=F32)
    x1_ref[...] = x1
    ms = jnp.mean(x1 * x1, axis=-1, keepdims=True)
    h2 = x1 * lax.rsqrt(ms + EPS) * g2_ref[...]
    h2_hi = h2.astype(BF16)
    _to_slabs(h2s_ref, h2_hi.astype(F32), tm)
    h2_lo = (h2 - h2_hi.astype(F32)).astype(BF16)
    logits = (jnp.dot(h2_hi, rw_hi_ref[...], preferred_element_type=F32)
              + jnp.dot(h2_lo, rw_hi_ref[...], preferred_element_type=F32)
              + jnp.dot(h2_hi, rw_lo_ref[...], preferred_element_type=F32)
              + rb_ref[...])
    lane = lax.broadcasted_iota(jnp.int32, logits.shape, 1)
    work = logits
    top = None
    topi = jnp.zeros(logits.shape, jnp.int32)
    topw = jnp.zeros_like(logits)
    chosen = jnp.zeros_like(logits)
    for k in range(TOP_K):
        mx = jnp.max(work, axis=1, keepdims=True)
        idx = jnp.min(jnp.where(work == mx, lane, LANES), axis=1, keepdims=True)
        sel = lane == idx
        if k == 0:
            top = mx
        topi = jnp.where(lane == k, idx, topi)
        topw = jnp.where(lane == k, jnp.exp(mx - top), topw)
        chosen = jnp.where(sel, 1.0, chosen)
        work = jnp.where(sel, -jnp.inf, work)
    topi_ref[...] = topi
    topw_ref[...] = topw / jnp.sum(topw, axis=1, keepdims=True)
    cnt_ref[0] = jnp.sum(chosen, axis=0, keepdims=True)


def _mix(x2, c, o, gates, wc, wa, wo, g2, rw_hi, rw_lo, rb, *, tm):
    t, d = x2.shape
    row = lambda n: pl.BlockSpec((tm, n), lambda i: (i, 0))
    return pl.pallas_call(
        functools.partial(_mix_kernel, d=d, tm=tm),
        grid=(t // tm,),
        in_specs=[row(d), row(d), row(d), row(2 * d), _resident((d, d)), _resident((d, d)), _resident((d, d)),
                  _resident((1, d)), _resident((d, LANES)), _resident((d, LANES)), _resident((1, LANES))],
        out_specs=[row(d), pl.BlockSpec((tm * SLAB, LANES), lambda i: (i, 0)), row(LANES), row(LANES),
                   pl.BlockSpec((1, 1, LANES), lambda i: (i, 0, 0))],
        out_shape=[jax.ShapeDtypeStruct((t, d), F32), jax.ShapeDtypeStruct((t * SLAB, LANES), F32),
                   jax.ShapeDtypeStruct((t, LANES), jnp.int32), jax.ShapeDtypeStruct((t, LANES), F32),
                   jax.ShapeDtypeStruct((t // tm, 1, LANES), F32)],
        compiler_params=_params(("parallel",)),
        name="mix_router",
    )(x2, c, o, gates, wc, wa, wo, g2, rw_hi, rw_lo, rb)


def _plan_kernel(topi_ref, base_ref, dest_ref, *, tt):
    idx = topi_ref[...]
    lane = lax.broadcasted_iota(jnp.int32, idx.shape, 1)
    onehots = [lane == idx[:, k:k + 1] for k in range(TOP_K)]
    chosen = jnp.zeros(idx.shape, F32)
    for oh in onehots:
        chosen = jnp.where(oh, 1.0, chosen)
    r_iota = lax.broadcasted_iota(jnp.int32, (tt, tt), 0)
    c_iota = lax.broadcasted_iota(jnp.int32, (tt, tt), 1)
    earlier = jnp.where(c_iota < r_iota, 1.0, 0.0).astype(BF16)
    rank = jnp.dot(earlier, chosen.astype(BF16), preferred_element_type=F32)
    pos = rank + base_ref[0]
    dest = jnp.zeros(idx.shape, F32)
    for k, oh in enumerate(onehots):
        dk = jnp.sum(jnp.where(oh, pos, 0.0), axis=1, keepdims=True)
        dest = jnp.where(lane == k, dk, dest)
    dest_ref[...] = dest.T[0:SLAB, :].astype(jnp.int32)


def _plan(topi, base, *, tt):
    t = topi.shape[0]
    nt = t // tt
    return pl.pallas_call(
        functools.partial(_plan_kernel, tt=tt),
        grid=(nt,),
        in_specs=[pl.BlockSpec((tt, LANES), lambda i: (i, 0)), pl.BlockSpec((1, 1, LANES), lambda i: (i, 0, 0))],
        out_specs=pl.BlockSpec((SLAB, tt), lambda i: (i, 0)),
        out_shape=jax.ShapeDtypeStruct((nt * SLAB, tt), jnp.int32),
        compiler_params=_params(("parallel",)),
        name="moe_plan",
    )(topi, base)


def _row_copy(src_ref, src_row, dst_ref, dst_row, sem):
    return pltpu.make_async_copy(src_ref.at[pl.ds(pl.multiple_of(src_row * SLAB, SLAB), SLAB)],
                                 dst_ref.at[pl.ds(pl.multiple_of(dst_row * SLAB, SLAB), SLAB)], sem)


def _pad_fill_copies(pad_start_ref, pad_cnt_ref, zero_ref, xs_hbm, sem, e, tr):
    cnt = pad_cnt_ref[e]
    out = []
    bit = 1
    while bit < tr:
        first = pad_start_ref[e] + (cnt & (bit - 1))
        copy = pltpu.make_async_copy(zero_ref.at[pl.ds(0, bit * SLAB)],
                                     xs_hbm.at[pl.ds(pl.multiple_of(first * SLAB, SLAB), bit * SLAB)], sem)
        out.append(((cnt & bit) != 0, copy))
        bit *= 2
    return out


def _dispatch_kernel(pad_start_ref, pad_cnt_ref, dest_hbm, h2s_ref, xs_hbm, dest_smem, zero_ref, sem, isem, zsem,
                     *, tt, tr, n_e):
    i = pl.program_id(0)
    load = pltpu.make_async_copy(dest_hbm.at[pl.ds(pl.multiple_of(i * SLAB, SLAB), SLAB)], dest_smem, isem)
    load.start()

    @pl.when(i == 0)
    def _():
        zero_ref[...] = jnp.zeros_like(zero_ref)
        for e in range(n_e):
            for pred, copy in _pad_fill_copies(pad_start_ref, pad_cnt_ref, zero_ref, xs_hbm, zsem, e, tr):
                pl.when(pred)(copy.start)

    load.wait()

    def issue(g, carry):
        toks = [g * ISSUE_GROUP + j for j in range(ISSUE_GROUP)]
        rows = [[dest_smem[k, tok] for k in range(TOP_K)] for tok in toks]
        for tok, row in zip(toks, rows):
            for k in range(TOP_K):
                _row_copy(h2s_ref, tok, xs_hbm, row[k], sem).start()
        return carry
    lax.fori_loop(0, tt // ISSUE_GROUP, issue, 0)

    def drain(tok, carry):
        for k in range(TOP_K):
            _row_copy(h2s_ref, 0, xs_hbm, 0, sem).wait()
        return carry
    lax.fori_loop(0, tt, drain, 0, unroll=8)

    @pl.when(i == pl.num_programs(0) - 1)
    def _():
        for e in range(n_e):
            for pred, copy in _pad_fill_copies(pad_start_ref, pad_cnt_ref, zero_ref, xs_hbm, zsem, e, tr):
                pl.when(pred)(copy.wait)


def _dispatch(pad_start, pad_cnt, dest, h2s, *, rows, tt, tr):
    nt = dest.shape[0] // SLAB
    n_e = pad_start.shape[0]
    anyspec = pl.BlockSpec(memory_space=pl.ANY)
    return pl.pallas_call(
        functools.partial(_dispatch_kernel, tt=tt, tr=tr, n_e=n_e),
        grid_spec=pltpu.PrefetchScalarGridSpec(
            num_scalar_prefetch=2, grid=(nt,),
            in_specs=[anyspec, pl.BlockSpec((tt * SLAB, LANES), lambda i, ps, pc: (i, 0))], out_specs=anyspec,
            scratch_shapes=[pltpu.SMEM((SLAB, tt), jnp.int32), pltpu.VMEM((tr // 2 * SLAB, LANES), F32),
                            pltpu.SemaphoreType.DMA, pltpu.SemaphoreType.DMA, pltpu.SemaphoreType.DMA]),
        out_shape=jax.ShapeDtypeStruct((rows * SLAB, LANES), F32),
        compiler_params=_params(("arbitrary",)),
        name="moe_dispatch",
    )(pad_start, pad_cnt, dest, h2s)


def _ffn_kernel(te_ref, nu_ref, xs_ref, w1_ref, b1_ref, w2_ref, b2_ref, o_ref, *, tr, dff):
    @pl.when(pl.program_id(0) < nu_ref[0])
    def _():
        xs = _from_slabs(xs_ref, tr).astype(BF16)
        hid = jnp.dot(xs, w1_ref[0], preferred_element_type=F32) + b1_ref[0]
        glu = jnp.minimum(hid[:, :dff], SWIGLU_LIMIT)
        lin = jnp.clip(hid[:, dff:], -SWIGLU_LIMIT, SWIGLU_LIMIT)
        act = glu * _sigmoid(SWIGLU_ALPHA * glu) * (lin + 1.0)
        out = jnp.dot(act.astype(BF16), w2_ref[0], preferred_element_type=F32) + b2_ref[0]
        _to_slabs(o_ref, out, tr)


def _ffn(tile_expert, n_used, xs, w1, b1, w2, b2, *, tr):
    n_e, d, dff2 = w1.shape
    n_tiles = tile_expert.shape[0]
    rows = lambda r, te, nu: (jnp.minimum(r, nu[0] - 1), 0)
    per_e = lambda a, b: pl.BlockSpec((1, a, b), lambda r, te, nu: (te[r], 0, 0))
    return pl.pallas_call(
        functools.partial(_ffn_kernel, tr=tr, dff=dff2 // 2),
        grid_spec=pltpu.PrefetchScalarGridSpec(
            num_scalar_prefetch=2, grid=(n_tiles,),
            in_specs=[pl.BlockSpec((tr * SLAB, LANES), rows), per_e(d, dff2), per_e(1, dff2), per_e(dff2 // 2, d),
                      per_e(1, d)],
            out_specs=pl.BlockSpec((tr * SLAB, LANES), rows)),
        out_shape=jax.ShapeDtypeStruct(xs.shape, F32),
        compiler_params=_params(("arbitrary",)),
        name="moe_experts",
    )(tile_expert, n_used, xs, w1, b1, w2, b2)


def _combine_kernel(dest_hbm, ys_hbm, x1_ref, topw_ref, fg_ref, o_ref, dest_smem, buf_ref, sem, isem, *, tt):
    i = pl.program_id(0)
    load = pltpu.make_async_copy(dest_hbm.at[pl.ds(pl.multiple_of(i * SLAB, SLAB), SLAB)], dest_smem, isem)
    load.start()
    load.wait()

    def issue(g, carry):
        toks = [g * ISSUE_GROUP + j for j in range(ISSUE_GROUP)]
        rows = [[dest_smem[k, tok] for k in range(TOP_K)] for tok in toks]
        for tok, row in zip(toks, rows):
            for k in range(TOP_K):
                _row_copy(ys_hbm, row[k], buf_ref.at[k], tok, sem).start()
        return carry
    lax.fori_loop(0, tt // ISSUE_GROUP, issue, 0)

    def drain(tok, carry):
        for k in range(TOP_K):
            _row_copy(ys_hbm, 0, buf_ref.at[k], 0, sem).wait()
        return carry
    lax.fori_loop(0, tt, drain, 0, unroll=8)

    y = x1_ref[...]
    w = topw_ref[...]
    for k in range(TOP_K):
        y = y + w[:, k:k + 1] * _from_slabs(buf_ref, tt, lead=(k,))
    ms = jnp.mean(y * y, axis=-1, keepdims=True)
    o_ref[...] = y * lax.rsqrt(ms + EPS) * fg_ref[...]


def _combine(dest, ys, x1, topw, fg, *, tt):
    t, d = x1.shape
    row = lambda n: pl.BlockSpec((tt, n), lambda i: (i, 0))
    anyspec = pl.BlockSpec(memory_space=pl.ANY)
    return pl.pallas_call(
        functools.partial(_combine_kernel, tt=tt),
        grid=(t // tt,),
        in_specs=[anyspec, anyspec, row(d), row(LANES), _resident((1, d))],
        out_specs=row(d),
        out_shape=jax.ShapeDtypeStruct((t, d), F32),
        scratch_shapes=[pltpu.SMEM((SLAB, tt), jnp.int32), pltpu.VMEM((TOP_K, tt * SLAB, LANES), F32),
                        pltpu.SemaphoreType.DMA, pltpu.SemaphoreType.DMA],
        compiler_params=_params(("arbitrary",)),
        name="moe_combine",
    )(dest, ys, x1, topw, fg)


def _moe(x1, h2s, topi, topw, counts, w1, b1, w2, b2, fg, *, tt, tr):
    t = x1.shape[0]
    n_e = w1.shape[0]
    cnt = counts[:, 0, :n_e].astype(jnp.int32)
    nt = cnt.shape[0]
    total = jnp.sum(cnt, axis=0)
    padded = (total + tr - 1) // tr * tr
    e_ids = jnp.arange(n_e)
    ends = jnp.sum(jnp.where(e_ids[:, None] <= e_ids[None, :], padded[:, None], 0), axis=0)
    starts = ends - padded
    t_ids = jnp.arange(nt)
    before = jnp.sum(jnp.where((t_ids[:, None] < t_ids[None, :])[:, :, None], cnt[:, None, :], 0), axis=0)
    base = starts[None, :] + before
    base = jnp.pad(base.astype(F32), ((0, 0), (0, LANES - n_e)))[:, None, :]
    n_tiles = (t * TOP_K) // tr + n_e
    n_used = (ends[-1] // tr).astype(jnp.int32)
    r = jnp.minimum(jnp.arange(n_tiles, dtype=jnp.int32), n_used - 1)
    tile_expert = jnp.sum((ends // tr)[None, :] <= r[:, None], axis=1).astype(jnp.int32)

    dest = _plan(topi, base, tt=tt)
    xs = _dispatch(starts + total, padded - total, dest, h2s, rows=n_tiles * tr, tt=tt, tr=tr)
    ys = _ffn(tile_expert, n_used.reshape(1), xs, w1, b1, w2, b2, tr=tr)
    return _combine(dest, ys, x1, topw, fg, tt=tt)


def _tile(n, want):
    want = min(want, n)
    assert n % want == 0, (n, want)
    return want


def kernel(x, norm1_g, w_in, conv_dw_w, conv_dw_b, conv_norm_g, conv_norm_b, w_conv_out, w_attn_out, gate_b,
           w_out, norm2_g, router_w, router_b, expert_w1, expert_b1, expert_w2, expert_b2, final_norm_g):
    batch, seq, d = x.shape
    depth = w_in.shape[0]
    t = batch * seq
    assert d == SLAB * LANES and N_EXPERTS <= LANES
    assert depth == 1, "the final rms_norm is fused into the combine kernel of the only layer"
    row2 = lambda a: a.reshape(1, -1)
    l = 0

    tb = _tile(seq, 256)
    tt = _tile(t, 512)
    x2 = x.reshape(t, d)
    u, q, k, v, gates = _in_proj(x2, row2(norm1_g[l]), w_in[l].astype(BF16), row2(gate_b[l]), tm=_tile(t, 512))
    o = _attention(q, k, v, batch=batch, seq=seq, tb=tb, heads=ATTN_HEADS_PER_STEP)
    c = _conv_branch(u, conv_dw_w[l], row2(conv_dw_b[l]), row2(conv_norm_g[l]), row2(conv_norm_b[l]),
                     batch=batch, seq=seq, ts=_tile(seq, 256))
    rw = jnp.pad(router_w[l], ((0, 0), (0, LANES - N_EXPERTS)))
    rw_hi = rw.astype(BF16)
    rw_lo = (rw - rw_hi.astype(F32)).astype(BF16)
    rb = jnp.pad(router_b[l].astype(F32), (0, LANES - N_EXPERTS), constant_values=-1e30).reshape(1, LANES)
    x1, h2s, topi, topw, counts = _mix(x2, c, o, gates, w_conv_out[l].astype(BF16), w_attn_out[l].astype(BF16),
                                       w_out[l].astype(BF16), row2(norm2_g[l]), rw_hi, rw_lo, rb, tm=tt)
    out = _moe(x1, h2s, topi, topw, counts, expert_w1[l].astype(BF16), expert_b1[l][:, None, :],
               expert_w2[l].astype(BF16), expert_b2[l][:, None, :], row2(final_norm_g), tt=tt,
               tr=_tile(t * TOP_K, 512))
    return out.reshape(batch, seq, d)
```

```python
import functools

import jax
import jax.numpy as jnp
from jax import lax
from jax.experimental import pallas as pl
from jax.experimental.pallas import tpu as pltpu

F32 = jnp.float32
BF16 = jnp.bfloat16

HEAD_DIM = 64
CONV_WIDTH = 31
N_EXPERTS = 32
TOP_K = 4
SWIGLU_LIMIT = 7.0
SWIGLU_ALPHA = 1.702
EPS = 1e-6

LANES = 128
SLAB = 8
ATTN_HEADS_PER_STEP = 4
ATTN_LEAD_ROWS = 160
ISSUE_GROUP = 8
CONV_HALO = 32
CONV_ROWS = 128
VMEM_LIMIT = 56 * 1024 * 1024

LOG2_E = 1.4426950408889634
SKIP_THRESHOLD = 150.0 * (1.0 + 2.0 ** -7)
SIGN_BIT = 0x80000000


def _params(sem):
    return pltpu.CompilerParams(dimension_semantics=sem, vmem_limit_bytes=VMEM_LIMIT)


def _resident(shape):
    return pl.BlockSpec(shape, lambda *_: (0,) * len(shape), pipeline_mode=pl.Buffered(1))


def _sigmoid(x):
    return 1.0 / (1.0 + jnp.exp(-x))


def _inproj_kernel(x_ref, g_ref, w_ref, gb_ref, u_ref, q_ref, k_ref, v_ref, gate_ref, *, d):
    x = x_ref[...]
    ms = jnp.mean(x * x, axis=-1, keepdims=True)
    h = (x * lax.rsqrt(ms + EPS) * g_ref[...]).astype(BF16)

    def proj(c):
        return jnp.dot(h, w_ref[:, c * d:(c + 1) * d], preferred_element_type=F32)

    u_ref[...] = (proj(0) * _sigmoid(proj(1))).astype(BF16)
    q_ref[...] = (proj(2) * (HEAD_DIM ** -0.5 * LOG2_E)).astype(BF16)
    k_ref[...] = proj(3).astype(BF16)
    v_ref[...] = proj(4).astype(BF16)
    gate_ref[:, :d] = _sigmoid(proj(5) + gb_ref[:, :d]).astype(BF16)
    gate_ref[:, d:] = _sigmoid(proj(6) + gb_ref[:, d:]).astype(BF16)


def _in_proj(x2, norm_g, w_in, gate_b, *, tm):
    t, d = x2.shape
    row = lambda n: pl.BlockSpec((tm, n), lambda i: (i, 0))
    out = lambda n: jax.ShapeDtypeStruct((t, n), BF16)
    return pl.pallas_call(
        functools.partial(_inproj_kernel, d=d),
        grid=(t // tm,),
        in_specs=[row(d), _resident((1, d)), _resident(w_in.shape), _resident((1, 2 * d))],
        out_specs=[row(d), row(d), row(d), row(d), row(2 * d)],
        out_shape=[out(d), out(d), out(d), out(d), out(2 * d)],
        compiler_params=_params(("parallel",)),
        name="in_proj",
    )(x2, norm_g, w_in, gate_b)


def _attn_kernel(q_ref, k_ref, v_ref, o_ref, *, tb, heads):
    i = pl.program_id(2)
    lead_rows = min(tb, ATTN_LEAD_ROWS)
    r_iota = lax.broadcasted_iota(jnp.int32, (tb, tb), 0)
    c_iota = lax.broadcasted_iota(jnp.int32, (tb, tb), 1)
    upper = jnp.where(r_iota > c_iota, 1.0, 0.0).astype(BF16)
    causal = c_iota < r_iota

    first_head = lax.broadcasted_iota(jnp.int32, (tb, LANES), 1) < HEAD_DIM
    zeros = jnp.zeros((tb, LANES), BF16)

    def split_heads(x2):
        return jnp.concatenate([jnp.where(first_head, x2, zeros), jnp.where(first_head, zeros, x2)], axis=0)

    def pair_block(p, j, carry, acc, diagonal, m):
        lanes = slice(p * LANES, (p + 1) * LANES)
        start = pl.multiple_of(j * tb, tb)
        z = lax.dot_general(q_ref[0:m, lanes], split_heads(k_ref[pl.ds(start, tb), lanes]),
                            (((1,), (1,)), ((), ())), preferred_element_type=F32)
        neg_abs = pltpu.bitcast(pltpu.bitcast(z, jnp.uint32) | jnp.uint32(SIGN_BIT), F32)
        l = jnp.log2(1.0 + jnp.exp2(neg_abs))
        sp = jnp.maximum(z, 0.0) + l
        logsig = jnp.minimum(z, 0.0) - l
        if diagonal:
            sp = jnp.where(causal2, sp, 0.0)
        spb = sp.astype(BF16)
        prefix = jnp.dot(jnp.concatenate([spb[:, :tb], spb[:, tb:]], axis=0), upper,
                         preferred_element_type=F32)
        mass = (prefix[0:m, 0:1] + sp[:, 0:1], prefix[m:, 0:1] + sp[:, tb:tb + 1])
        later = jnp.concatenate([prefix[0:m] + carry[0][0:m], prefix[m:] + carry[1][0:m]], axis=1)
        a = jnp.exp2(logsig - later)
        if diagonal:
            a = jnp.where(causal2, a, 0.0)
        update = jnp.dot(a.astype(BF16), split_heads(v_ref[pl.ds(start, tb), lanes]), preferred_element_type=F32)
        if m == tb:
            return (carry[0] + mass[0], carry[1] + mass[1]), acc + update
        keep = lambda new, old: jnp.concatenate([new, old[m:]], axis=0)
        return ((keep(carry[0][0:m] + mass[0], carry[0]), keep(carry[1][0:m] + mass[1], carry[1])),
                keep(acc[0:m] + update, acc))

    pairs = heads * HEAD_DIM // LANES
    causal2 = jnp.concatenate([causal, causal], axis=1)

    def blocks(j, carries, accs, diagonal, m):
        new = [pair_block(p, j, carries[p], accs[p], diagonal, m) for p in range(pairs)]
        return tuple(c for c, _ in new), tuple(a for _, a in new)

    zero_carry = (jnp.zeros((tb, 1), F32), jnp.zeros((tb, 1), F32))
    carries, accs = blocks(i, (zero_carry,) * pairs, (jnp.zeros((tb, LANES), F32),) * pairs, True, tb)

    def least_mass(carries, rows):
        return jnp.min(functools.reduce(jnp.minimum, [c[rows] for pair in carries for c in pair]))

    def cond(state):
        j, carries, _ = state
        return jnp.logical_and(j >= 0, least_mass(carries, slice(None)) < SKIP_THRESHOLD)

    def body(state):
        j, carries, accs = state
        tail_done = least_mass(carries, slice(lead_rows, None)) >= SKIP_THRESHOLD
        carries, accs = lax.cond(tail_done,
                                 lambda c, a: blocks(j, c, a, False, lead_rows),
                                 lambda c, a: blocks(j, c, a, False, tb), carries, accs)
        return j - 1, carries, accs

    _, _, accs = lax.while_loop(cond, body, (i - 1, carries, accs))
    for p in range(pairs):
        o_ref[:, p * LANES:(p + 1) * LANES] = accs[p].astype(BF16)


def _attention(q, k, v, *, batch, seq, tb, heads):
    t, d = q.shape
    nq = seq // tb
    width = heads * HEAD_DIM
    qspec = pl.BlockSpec((tb, width), lambda b, hp, i: (b * nq + i, hp))
    kvspec = pl.BlockSpec((seq, width), lambda b, hp, i: (b, hp))
    return pl.pallas_call(
        functools.partial(_attn_kernel, tb=tb, heads=heads),
        grid=(batch, d // width, nq),
        in_specs=[qspec, kvspec, kvspec],
        out_specs=qspec,
        out_shape=jax.ShapeDtypeStruct((t, d), BF16),
        compiler_params=_params(("parallel", "parallel", "arbitrary")),
        name="attention",
    )(q, k, v)


def _conv_kernel(prev_ref, cur_ref, w_ref, b_ref, g_ref, nb_ref, o_ref, win_ref, conv_ref, *, ts, d):
    i = pl.program_id(1)
    prev = prev_ref[...].astype(F32)
    win_ref[0:CONV_HALO, :] = jnp.where(i == 0, 0.0, prev)
    win_ref[CONV_HALO:, :] = cur_ref[...].astype(F32)
    base = CONV_HALO - (CONV_WIDTH - 1)
    rows = CONV_ROWS
    span = rows + CONV_HALO
    for c in range(d // LANES):
        lanes = slice(c * LANES, (c + 1) * LANES)
        for r0 in range(0, ts, rows):
            window = win_ref[r0:r0 + span, lanes]
            acc = jnp.zeros((rows, LANES), F32)
            for shift in range(SLAB):
                shifted = pltpu.roll(window, (span - shift) % span, axis=0) if shift else window
                for w in range(CONV_WIDTH):
                    if (base + w) % SLAB == shift:
                        lead = (base + w) // SLAB * SLAB
                        acc = acc + shifted[lead:lead + rows] * w_ref[w:w + 1, lanes]
            conv_ref[r0:r0 + rows, lanes] = acc + b_ref[:, lanes]
    y = conv_ref[...]
    mu = jnp.mean(y, axis=-1, keepdims=True)
    yc = y - mu
    var = jnp.mean(yc * yc, axis=-1, keepdims=True)
    yn = yc * lax.rsqrt(var + EPS) * g_ref[...] + nb_ref[...]
    o_ref[...] = (yn * _sigmoid(yn)).astype(BF16)


def _conv_branch(u, dw_w, dw_b, norm_g, norm_b, *, batch, seq, ts):
    t, d = u.shape
    ns = seq // ts
    per = ts // CONV_HALO
    cur = pl.BlockSpec((ts, d), lambda b, i: (b * ns + i, 0))
    prev = pl.BlockSpec((CONV_HALO, d), lambda b, i: (jnp.maximum((b * ns + i) * per - 1, 0), 0))
    return pl.pallas_call(
        functools.partial(_conv_kernel, ts=ts, d=d),
        grid=(batch, ns),
        in_specs=[prev, cur, _resident(dw_w.shape), _resident((1, d)), _resident((1, d)), _resident((1, d))],
        out_specs=cur,
        out_shape=jax.ShapeDtypeStruct((t, d), BF16),
        scratch_shapes=[pltpu.VMEM((ts + CONV_HALO, d), F32), pltpu.VMEM((ts, d), F32)],
        compiler_params=_params(("parallel", "parallel")),
        name="conv_branch",
    )(u, u, dw_w, dw_b, norm_g, norm_b)


def _to_slabs(ref, val, rows):
    for c in range(SLAB):
        ref[pl.ds(c, rows, stride=SLAB), :] = val[:, c * LANES:(c + 1) * LANES]


def _from_slabs(ref, rows, lead=()):
    return jnp.concatenate([ref[lead + (pl.ds(c, rows, stride=SLAB), slice(None))] for c in range(SLAB)], axis=1)


def _mix_kernel(x_ref, c_ref, o_ref, gate_ref, wc_ref, wa_ref, wo_ref, g2_ref, rw_hi_ref, rw_lo_ref, rb_ref,
                x1_ref, h2s_ref, topi_ref, topw_ref, cnt_ref, *, d, tm):
    y_conv = jnp.dot(c_ref[...], wc_ref[...], preferred_element_type=F32)
    y_attn = jnp.dot(o_ref[...], wa_ref[...], preferred_element_type=F32)
    m = gate_ref[:, :d].astype(F32) * y_conv + gate_ref[:, d:].astype(F32) * y_attn
    x1 = x_ref[...] + jnp.dot(m.astype(BF16), wo_ref[...], preferred_element_type=F32)
    x1_ref[...] = x1
    ms = jnp.mean(x1 * x1, axis=-1, keepdims=True)
    h2 = x1 * lax.rsqrt(ms + EPS) * g2_ref[...]
    h2_hi = h2.astype(BF16)
    _to_slabs(h2s_ref, h2_hi.astype(F32), tm)
    h2_lo = (h2 - h2_hi.astype(F32)).astype(BF16)
    logits = (jnp.dot(h2_hi, rw_hi_ref[...], preferred_element_type=F32)
              + jnp.dot(h2_lo, rw_hi_ref[...], preferred_element_type=F32)
              + jnp.dot(h2_hi, rw_lo_ref[...], preferred_element_type=F32)
              + rb_ref[...])
    lane = lax.broadcasted_iota(jnp.int32, logits.shape, 1)
    work = logits
    top = None
    topi = jnp.zeros(logits.shape, jnp.int32)
    topw = jnp.zeros_like(logits)
    chosen = jnp.zeros_like(logits)
    for k in range(TOP_K):
        mx = jnp.max(work, axis=1, keepdims=True)
        idx = jnp.min(jnp.where(work == mx, lane, LANES), axis=1, keepdims=True)
        sel = lane == idx
        if k == 0:
            top = mx
        topi = jnp.where(lane == k, idx, topi)
        topw = jnp.where(lane == k, jnp.exp(mx - top), topw)
        chosen = jnp.where(sel, 1.0, chosen)
        work = jnp.where(sel, -jnp.inf, work)
    topi_ref[...] = topi
    topw_ref[...] = topw / jnp.sum(topw, axis=1, keepdims=True)
    cnt_ref[0] = jnp.sum(chosen, axis=0, keepdims=True)


def _mix(x2, c, o, gates, wc, wa, wo, g2, rw_hi, rw_lo, rb, *, tm):
    t, d = x2.shape
    row = lambda n: pl.BlockSpec((tm, n), lambda i: (i, 0))
    return pl.pallas_call(
        functools.partial(_mix_kernel, d=d, tm=tm),
        grid=(t // tm,),
        in_specs=[row(d), row(d), row(d), row(2 * d), _resident((d, d)), _resident((d, d)), _resident((d, d)),
                  _resident((1, d)), _resident((d, LANES)), _resident((d, LANES)), _resident((1, LANES))],
        out_specs=[row(d), pl.BlockSpec((tm * SLAB, LANES), lambda i: (i, 0)), row(LANES), row(LANES),
                   pl.BlockSpec((1, 1, LANES), lambda i: (i, 0, 0))],
        out_shape=[jax.ShapeDtypeStruct((t, d), F32), jax.ShapeDtypeStruct((t * SLAB, LANES), F32),
                   jax.ShapeDtypeStruct((t, LANES), jnp.int32), jax.ShapeDtypeStruct((t, LANES), F32),
                   jax.ShapeDtypeStruct((t // tm, 1, LANES), F32)],
        compiler_params=_params(("parallel",)),
        name="mix_router",
    )(x2, c, o, gates, wc, wa, wo, g2, rw_hi, rw_lo, rb)


def _plan_kernel(topi_ref, base_ref, dest_ref, *, tt):
    idx = topi_ref[...]
    lane = lax.broadcasted_iota(jnp.int32, idx.shape, 1)
    onehots = [lane == idx[:, k:k + 1] for k in range(TOP_K)]
    chosen = jnp.zeros(idx.shape, F32)
    for oh in onehots:
        chosen = jnp.where(oh, 1.0, chosen)
    r_iota = lax.broadcasted_iota(jnp.int32, (tt, tt), 0)
    c_iota = lax.broadcasted_iota(jnp.int32, (tt, tt), 1)
    earlier = jnp.where(c_iota < r_iota, 1.0, 0.0).astype(BF16)
    rank = jnp.dot(earlier, chosen.astype(BF16), preferred_element_type=F32)
    pos = rank + base_ref[0]
    dest = jnp.zeros(idx.shape, F32)
    for k, oh in enumerate(onehots):
        dk = jnp.sum(jnp.where(oh, pos, 0.0), axis=1, keepdims=True)
        dest = jnp.where(lane == k, dk, dest)
    dest_ref[...] = dest.T[0:SLAB, :].astype(jnp.int32)


def _plan(topi, base, *, tt):
    t = topi.shape[0]
    nt = t // tt
    return pl.pallas_call(
        functools.partial(_plan_kernel, tt=tt),
        grid=(nt,),
        in_specs=[pl.BlockSpec((tt, LANES), lambda i: (i, 0)), pl.BlockSpec((1, 1, LANES), lambda i: (i, 0, 0))],
        out_specs=pl.BlockSpec((SLAB, tt), lambda i: (i, 0)),
        out_shape=jax.ShapeDtypeStruct((nt * SLAB, tt), jnp.int32),
        compiler_params=_params(("parallel",)),
        name="moe_plan",
    )(topi, base)


def _row_copy(src_ref, src_row, dst_ref, dst_row, sem):
    return pltpu.make_async_copy(src_ref.at[pl.ds(pl.multiple_of(src_row * SLAB, SLAB), SLAB)],
                                 dst_ref.at[pl.ds(pl.multiple_of(dst_row * SLAB, SLAB), SLAB)], sem)


def _dest_load(dest_hbm, dest_smem, isem, tile, slot):
    return pltpu.make_async_copy(dest_hbm.at[pl.ds(pl.multiple_of(tile * SLAB, SLAB), SLAB)], dest_smem.at[slot], isem)


def _pad_fill_copies(pad_start_ref, pad_cnt_ref, zero_ref, xs_hbm, sem, e, tr):
    cnt = pad_cnt_ref[e]
    out = []
    bit = 1
    while bit < tr:
        first = pad_start_ref[e] + (cnt & (bit - 1))
        copy = pltpu.make_async_copy(zero_ref.at[pl.ds(0, bit * SLAB)],
                                     xs_hbm.at[pl.ds(pl.multiple_of(first * SLAB, SLAB), bit * SLAB)], sem)
        out.append(((cnt & bit) != 0, copy))
        bit *= 2
    return out


def _dispatch_kernel(pad_start_ref, pad_cnt_ref, dest_hbm, h2s_ref, xs_hbm, dest_smem, zero_ref, sem, isem, zsem,
                     *, tt, tr, n_e):
    i = pl.program_id(0)
    slot = i % 2

    @pl.when(i == 0)
    def _():
        _dest_load(dest_hbm, dest_smem, isem, 0, 0).start()

    @pl.when(i == 0)
    def _():
        zero_ref[...] = jnp.zeros_like(zero_ref)
        for e in range(n_e):
            for pred, copy in _pad_fill_copies(pad_start_ref, pad_cnt_ref, zero_ref, xs_hbm, zsem, e, tr):
                pl.when(pred)(copy.start)

    _dest_load(dest_hbm, dest_smem, isem, i, slot).wait()

    @pl.when(i + 1 < pl.num_programs(0))
    def _():
        _dest_load(dest_hbm, dest_smem, isem, i + 1, 1 - slot).start()

    def issue(g, carry):
        toks = [g * ISSUE_GROUP + j for j in range(ISSUE_GROUP)]
        rows = [[dest_smem[slot, k, tok] for k in range(TOP_K)] for tok in toks]
        for m, (tok, row) in enumerate(zip(toks, rows)):
            for k in range(TOP_K):
                _row_copy(h2s_ref, tok, xs_hbm, row[k], sem).start(priority=(m * TOP_K + k) % 2)
        return carry
    lax.fori_loop(0, tt // ISSUE_GROUP, issue, 0)

    def drain(tok, carry):
        for k in range(TOP_K):
            _row_copy(h2s_ref, 0, xs_hbm, 0, sem).wait()
        return carry
    lax.fori_loop(0, tt, drain, 0, unroll=8)

    @pl.when(i == pl.num_programs(0) - 1)
    def _():
        for e in range(n_e):
            for pred, copy in _pad_fill_copies(pad_start_ref, pad_cnt_ref, zero_ref, xs_hbm, zsem, e, tr):
                pl.when(pred)(copy.wait)


def _dispatch(pad_start, pad_cnt, dest, h2s, *, rows, tt, tr):
    nt = dest.shape[0] // SLAB
    n_e = pad_start.shape[0]
    anyspec = pl.BlockSpec(memory_space=pl.ANY)
    return pl.pallas_call(
        functools.partial(_dispatch_kernel, tt=tt, tr=tr, n_e=n_e),
        grid_spec=pltpu.PrefetchScalarGridSpec(
            num_scalar_prefetch=2, grid=(nt,),
            in_specs=[anyspec, pl.BlockSpec((tt * SLAB, LANES), lambda i, ps, pc: (i, 0))], out_specs=anyspec,
            scratch_shapes=[pltpu.SMEM((2, SLAB, tt), jnp.int32), pltpu.VMEM((tr // 2 * SLAB, LANES), F32),
                            pltpu.SemaphoreType.DMA, pltpu.SemaphoreType.DMA, pltpu.SemaphoreType.DMA]),
        out_shape=jax.ShapeDtypeStruct((rows * SLAB, LANES), F32),
        compiler_params=_params(("arbitrary",)),
        name="moe_dispatch",
    )(pad_start, pad_cnt, dest, h2s)


def _ffn_kernel(te_ref, nu_ref, xs_ref, w1_ref, b1_ref, w2_ref, b2_ref, o_ref, w1b_ref, w2b_ref, *, tr, dff):
    r = pl.program_id(0)

    @pl.when(jnp.logical_or(r == 0, te_ref[r] != te_ref[jnp.maximum(r - 1, 0)]))
    def _():
        w1b_ref[...] = w1_ref[0].astype(BF16)
        w2b_ref[...] = w2_ref[0].astype(BF16)

    @pl.when(r < nu_ref[0])
    def _():
        xs = _from_slabs(xs_ref, tr).astype(BF16)
        hid = jnp.dot(xs, w1b_ref[...], preferred_element_type=F32) + b1_ref[0]
        glu = jnp.minimum(hid[:, :dff], SWIGLU_LIMIT)
        lin = jnp.clip(hid[:, dff:], -SWIGLU_LIMIT, SWIGLU_LIMIT)
        act = glu * _sigmoid(SWIGLU_ALPHA * glu) * (lin + 1.0)
        out = jnp.dot(act.astype(BF16), w2b_ref[...], preferred_element_type=F32) + b2_ref[0]
        _to_slabs(o_ref, out, tr)


def _ffn(tile_expert, n_used, xs, w1, b1, w2, b2, *, tr):
    n_e, d, dff2 = w1.shape
    n_tiles = tile_expert.shape[0]
    rows = lambda r, te, nu: (jnp.minimum(r, nu[0] - 1), 0)
    per_e = lambda a, b: pl.BlockSpec((1, a, b), lambda r, te, nu: (te[r], 0, 0))
    return pl.pallas_call(
        functools.partial(_ffn_kernel, tr=tr, dff=dff2 // 2),
        grid_spec=pltpu.PrefetchScalarGridSpec(
            num_scalar_prefetch=2, grid=(n_tiles,),
            in_specs=[pl.BlockSpec((tr * SLAB, LANES), rows), per_e(d, dff2), per_e(1, dff2), per_e(dff2 // 2, d),
                      per_e(1, d)],
            out_specs=pl.BlockSpec((tr * SLAB, LANES), rows),
            scratch_shapes=[pltpu.VMEM((d, dff2), BF16), pltpu.VMEM((dff2 // 2, d), BF16)]),
        out_shape=jax.ShapeDtypeStruct(xs.shape, F32),
        compiler_params=_params(("arbitrary",)),
        name="moe_experts",
    )(tile_expert, n_used, xs, w1, b1, w2, b2)


def _combine_kernel(dest_hbm, ys_hbm, x1_ref, topw_ref, fg_ref, o_ref, dest_smem, buf_ref, sems, isem, *, tt):
    i = pl.program_id(0)
    n = pl.num_programs(0)

    def gather(tile):
        slot = tile % 2
        _dest_load(dest_hbm, dest_smem, isem, tile, slot).wait()

        @pl.when(tile + 1 < n)
        def _():
            _dest_load(dest_hbm, dest_smem, isem, tile + 1, 1 - slot).start()

        def issue(g, carry):
            toks = [g * ISSUE_GROUP + j for j in range(ISSUE_GROUP)]
            rows = [[dest_smem[slot, k, tok] for k in range(TOP_K)] for tok in toks]
            for m, (tok, row) in enumerate(zip(toks, rows)):
                for k in range(TOP_K):
                    _row_copy(ys_hbm, row[k], buf_ref.at[slot, k], tok, sems.at[slot]).start(
                        priority=(m * TOP_K + k) % 2)
            return carry
        lax.fori_loop(0, tt // ISSUE_GROUP, issue, 0)

    @pl.when(i == 0)
    def _():
        _dest_load(dest_hbm, dest_smem, isem, 0, 0).start()
        gather(0)

    @pl.when(i + 1 < n)
    def _():
        gather(i + 1)

    slot = i % 2

    def drain(tok, carry):
        for k in range(TOP_K):
            _row_copy(ys_hbm, 0, buf_ref.at[slot, k], 0, sems.at[slot]).wait()
        return carry
    lax.fori_loop(0, tt, drain, 0, unroll=8)

    y = x1_ref[...]
    w = topw_ref[...]
    for k in range(TOP_K):
        y = y + w[:, k:k + 1] * _from_slabs(buf_ref, tt, lead=(slot, k))
    ms = jnp.mean(y * y, axis=-1, keepdims=True)
    o_ref[...] = y * lax.rsqrt(ms + EPS) * fg_ref[...]


def _combine(dest, ys, x1, topw, fg, *, tt):
    t, d = x1.shape
    row = lambda n: pl.BlockSpec((tt, n), lambda i: (i, 0))
    anyspec = pl.BlockSpec(memory_space=pl.ANY)
    return pl.pallas_call(
        functools.partial(_combine_kernel, tt=tt),
        grid=(t // tt,),
        in_specs=[anyspec, anyspec, row(d), row(LANES), _resident((1, d))],
        out_specs=row(d),
        out_shape=jax.ShapeDtypeStruct((t, d), F32),
        scratch_shapes=[pltpu.SMEM((2, SLAB, tt), jnp.int32), pltpu.VMEM((2, TOP_K, tt * SLAB, LANES), F32),
                        pltpu.SemaphoreType.DMA((2,)), pltpu.SemaphoreType.DMA],
        compiler_params=_params(("arbitrary",)),
        name="moe_combine",
    )(dest, ys, x1, topw, fg)


def _moe(x1, h2s, topi, topw, counts, w1, b1, w2, b2, fg, *, tt, tr):
    t = x1.shape[0]
    n_e = w1.shape[0]
    cnt = counts[:, 0, :n_e].astype(jnp.int32)
    nt = cnt.shape[0]
    total = jnp.sum(cnt, axis=0)
    padded = (total + tr - 1) // tr * tr
    e_ids = jnp.arange(n_e)
    ends = jnp.sum(jnp.where(e_ids[:, None] <= e_ids[None, :], padded[:, None], 0), axis=0)
    starts = ends - padded
    t_ids = jnp.arange(nt)
    before = jnp.sum(jnp.where((t_ids[:, None] < t_ids[None, :])[:, :, None], cnt[:, None, :], 0), axis=0)
    base = starts[None, :] + before
    base = jnp.pad(base.astype(F32), ((0, 0), (0, LANES - n_e)))[:, None, :]
    n_tiles = (t * TOP_K) // tr + n_e
    n_used = (ends[-1] // tr).astype(jnp.int32)
    r = jnp.minimum(jnp.arange(n_tiles, dtype=jnp.int32), n_used - 1)
    tile_expert = jnp.sum((ends // tr)[None, :] <= r[:, None], axis=1).astype(jnp.int32)

    dest = _plan(topi, base, tt=tt)
    xs = _dispatch(starts + total, padded - total, dest, h2s, rows=n_tiles * tr, tt=tt, tr=tr)
    ys = _ffn(tile_expert, n_used.reshape(1), xs, w1, b1, w2, b2, tr=tr)
    return _combine(dest, ys, x1, topw, fg, tt=tt)


def _tile(n, want):
    want = min(want, n)
    assert n % want == 0, (n, want)
    return want


def kernel(x, norm1_g, w_in, conv_dw_w, conv_dw_b, conv_norm_g, conv_norm_b, w_conv_out, w_attn_out, gate_b,
           w_out, norm2_g, router_w, router_b, expert_w1, expert_b1, expert_w2, expert_b2, final_norm_g):
    batch, seq, d = x.shape
    depth = w_in.shape[0]
    t = batch * seq
    assert d == SLAB * LANES and N_EXPERTS <= LANES
    assert depth == 1, "the final rms_norm is fused into the combine kernel of the only layer"
    row2 = lambda a: a.reshape(1, -1)
    l = 0

    tb = _tile(seq, 256)
    tt = _tile(t, 512)
    x2 = x.reshape(t, d)
    u, q, k, v, gates = _in_proj(x2, row2(norm1_g[l]), w_in[l].astype(BF16), row2(gate_b[l]), tm=_tile(t, 512))
    o = _attention(q, k, v, batch=batch, seq=seq, tb=tb, heads=ATTN_HEADS_PER_STEP)
    c = _conv_branch(u, conv_dw_w[l], row2(conv_dw_b[l]), row2(conv_norm_g[l]), row2(conv_norm_b[l]),
                     batch=batch, seq=seq, ts=_tile(seq, 256))
    rw = jnp.pad(router_w[l], ((0, 0), (0, LANES - N_EXPERTS)))
    rw_hi = rw.astype(BF16)
    rw_lo = (rw - rw_hi.astype(F32)).astype(BF16)
    rb = jnp.pad(router_b[l].astype(F32), (0, LANES - N_EXPERTS), constant_values=-1e30).reshape(1, LANES)
    x1, h2s, topi, topw, counts = _mix(x2, c, o, gates, w_conv_out[l].astype(BF16), w_attn_out[l].astype(BF16),
                                       w_out[l].astype(BF16), row2(norm2_g[l]), rw_hi, rw_lo, rb, tm=tt)
    out = _moe(x1, h2s, topi, topw, counts, expert_w1[l], expert_b1[l][:, None, :],
               expert_w2[l], expert_b2[l][:, None, :], row2(final_norm_g), tt=tt,
               tr=_tile(t * TOP_K, 512))
    return out.reshape(batch, seq, d)
```

```python
import functools

import jax
import jax.numpy as jnp
from jax import lax
from jax.experimental import pallas as pl
from jax.experimental.pallas import tpu as pltpu

F32 = jnp.float32
BF16 = jnp.bfloat16

HEAD_DIM = 64
CONV_WIDTH = 31
N_EXPERTS = 32
TOP_K = 4
SWIGLU_LIMIT = 7.0
SWIGLU_ALPHA = 1.702
EPS = 1e-6

LANES = 128
SLAB = 8
ATTN_HEADS_PER_STEP = 4
ISSUE_GROUP = 8
CONV_HALO = 32
CONV_ROWS = 128
VMEM_LIMIT = 56 * 1024 * 1024

LOG2_E = 1.4426950408889634
SKIP_THRESHOLD = 150.0 * (1.0 + 2.0 ** -7)
SIGN_BIT = 0x80000000


def _params(sem):
    return pltpu.CompilerParams(dimension_semantics=sem, vmem_limit_bytes=VMEM_LIMIT)


def _resident(shape):
    return pl.BlockSpec(shape, lambda *_: (0,) * len(shape), pipeline_mode=pl.Buffered(1))


def _sigmoid(x):
    return 1.0 / (1.0 + jnp.exp(-x))


def _inproj_kernel(x_ref, g_ref, w_ref, gb_ref, u_ref, q_ref, k_ref, v_ref, gate_ref, *, d):
    x = x_ref[...]
    ms = jnp.mean(x * x, axis=-1, keepdims=True)
    h = (x * lax.rsqrt(ms + EPS) * g_ref[...]).astype(BF16)

    def proj(c):
        return jnp.dot(h, w_ref[:, c * d:(c + 1) * d], preferred_element_type=F32)

    u_ref[...] = (proj(0) * _sigmoid(proj(1))).astype(BF16)
    q_ref[...] = (proj(2) * (HEAD_DIM ** -0.5 * LOG2_E)).astype(BF16)
    k_ref[...] = proj(3).astype(BF16)
    v_ref[...] = proj(4).astype(BF16)
    gate_ref[:, :d] = _sigmoid(proj(5) + gb_ref[:, :d]).astype(BF16)
    gate_ref[:, d:] = _sigmoid(proj(6) + gb_ref[:, d:]).astype(BF16)


def _in_proj(x2, norm_g, w_in, gate_b, *, tm):
    t, d = x2.shape
    row = lambda n: pl.BlockSpec((tm, n), lambda i: (i, 0))
    out = lambda n: jax.ShapeDtypeStruct((t, n), BF16)
    return pl.pallas_call(
        functools.partial(_inproj_kernel, d=d),
        grid=(t // tm,),
        in_specs=[row(d), _resident((1, d)), _resident(w_in.shape), _resident((1, 2 * d))],
        out_specs=[row(d), row(d), row(d), row(d), row(2 * d)],
        out_shape=[out(d), out(d), out(d), out(d), out(2 * d)],
        compiler_params=_params(("parallel",)),
        name="in_proj",
    )(x2, norm_g, w_in, gate_b)


def _attn_kernel(q_ref, k_ref, v_ref, o_ref, *, tb, heads):
    i = pl.program_id(2)
    r_iota = lax.broadcasted_iota(jnp.int32, (tb, tb), 0)
    c_iota = lax.broadcasted_iota(jnp.int32, (tb, tb), 1)
    upper = jnp.where(r_iota > c_iota, 1.0, 0.0).astype(BF16)
    causal = c_iota < r_iota

    first_head = lax.broadcasted_iota(jnp.int32, (tb, LANES), 1) < HEAD_DIM
    zeros = jnp.zeros((tb, LANES), BF16)

    def split_heads(x2):
        return jnp.concatenate([jnp.where(first_head, x2, zeros), jnp.where(first_head, zeros, x2)], axis=0)

    def pair_block(p, j, carry, acc, diagonal):
        lanes = slice(p * LANES, (p + 1) * LANES)
        start = pl.multiple_of(j * tb, tb)
        z = lax.dot_general(q_ref[:, lanes], split_heads(k_ref[pl.ds(start, tb), lanes]),
                            (((1,), (1,)), ((), ())), preferred_element_type=F32)
        neg_abs = pltpu.bitcast(pltpu.bitcast(z, jnp.uint32) | jnp.uint32(SIGN_BIT), F32)
        l = jnp.log2(1.0 + jnp.exp2(neg_abs))
        sp = jnp.maximum(z, 0.0) + l
        logsig = jnp.minimum(z, 0.0) - l
        if diagonal:
            sp = jnp.where(causal2, sp, 0.0)
        spb = sp.astype(BF16)
        prefix = jnp.dot(jnp.concatenate([spb[:, :tb], spb[:, tb:]], axis=0), upper,
                         preferred_element_type=F32)
        mass = (prefix[0:tb, 0:1] + sp[:, 0:1], prefix[tb:, 0:1] + sp[:, tb:tb + 1])
        later = jnp.concatenate([prefix[0:tb] + carry[0], prefix[tb:] + carry[1]], axis=1)
        a = jnp.exp2(logsig - later)
        if diagonal:
            a = jnp.where(causal2, a, 0.0)
        acc = acc + jnp.dot(a.astype(BF16), split_heads(v_ref[pl.ds(start, tb), lanes]),
                            preferred_element_type=F32)
        return (carry[0] + mass[0], carry[1] + mass[1]), acc

    pairs = heads * HEAD_DIM // LANES
    causal2 = jnp.concatenate([causal, causal], axis=1)

    def blocks(j, carries, accs, diagonal):
        new = [pair_block(p, j, carries[p], accs[p], diagonal) for p in range(pairs)]
        return tuple(c for c, _ in new), tuple(a for _, a in new)

    zero_carry = (jnp.zeros((tb, 1), F32), jnp.zeros((tb, 1), F32))

    def leading_blocks(with_previous):
        carries, accs = blocks(i, (zero_carry,) * pairs, (jnp.zeros((tb, LANES), F32),) * pairs, True)
        if with_previous:
            carries, accs = blocks(i - 1, carries, accs, False)
        return carries, accs

    carries, accs = lax.cond(i > 0, lambda: leading_blocks(True), lambda: leading_blocks(False))

    def cond(state):
        j, carries, _ = state
        least = functools.reduce(jnp.minimum, [c for pair in carries for c in pair])
        return jnp.logical_and(j >= 0, jnp.min(least) < SKIP_THRESHOLD)

    def body(state):
        j, carries, accs = state
        carries, accs = blocks(j, carries, accs, False)
        return j - 1, carries, accs

    _, _, accs = lax.while_loop(cond, body, (i - 2, carries, accs))
    for p in range(pairs):
        o_ref[:, p * LANES:(p + 1) * LANES] = accs[p].astype(BF16)


def _attention(q, k, v, *, batch, seq, tb, heads):
    t, d = q.shape
    nq = seq // tb
    width = heads * HEAD_DIM
    qspec = pl.BlockSpec((tb, width), lambda b, hp, i: (b * nq + i, hp))
    kvspec = pl.BlockSpec((seq, width), lambda b, hp, i: (b, hp))
    return pl.pallas_call(
        functools.partial(_attn_kernel, tb=tb, heads=heads),
        grid=(batch, d // width, nq),
        in_specs=[qspec, kvspec, kvspec],
        out_specs=qspec,
        out_shape=jax.ShapeDtypeStruct((t, d), BF16),
        compiler_params=_params(("parallel", "parallel", "arbitrary")),
        name="attention",
    )(q, k, v)


def _conv_kernel(prev_ref, cur_ref, w_ref, b_ref, g_ref, nb_ref, o_ref, win_ref, conv_ref, *, ts, d):
    i = pl.program_id(1)
    prev = prev_ref[...].astype(F32)
    win_ref[0:CONV_HALO, :] = jnp.where(i == 0, 0.0, prev)
    win_ref[CONV_HALO:, :] = cur_ref[...].astype(F32)
    base = CONV_HALO - (CONV_WIDTH - 1)
    rows = CONV_ROWS
    span = rows + CONV_HALO
    for c in range(d // LANES):
        lanes = slice(c * LANES, (c + 1) * LANES)
        for r0 in range(0, ts, rows):
            window = win_ref[r0:r0 + span, lanes]
            acc = jnp.zeros((rows, LANES), F32)
            for shift in range(SLAB):
                shifted = pltpu.roll(window, (span - shift) % span, axis=0) if shift else window
                for w in range(CONV_WIDTH):
                    if (base + w) % SLAB == shift:
                        lead = (base + w) // SLAB * SLAB
                        acc = acc + shifted[lead:lead + rows] * w_ref[w:w + 1, lanes]
            conv_ref[r0:r0 + rows, lanes] = acc + b_ref[:, lanes]
    y = conv_ref[...]
    mu = jnp.mean(y, axis=-1, keepdims=True)
    yc = y - mu
    var = jnp.mean(yc * yc, axis=-1, keepdims=True)
    yn = yc * lax.rsqrt(var + EPS) * g_ref[...] + nb_ref[...]
    o_ref[...] = (yn * _sigmoid(yn)).astype(BF16)


def _conv_branch(u, dw_w, dw_b, norm_g, norm_b, *, batch, seq, ts):
    t, d = u.shape
    ns = seq // ts
    per = ts // CONV_HALO
    cur = pl.BlockSpec((ts, d), lambda b, i: (b * ns + i, 0))
    prev = pl.BlockSpec((CONV_HALO, d), lambda b, i: (jnp.maximum((b * ns + i) * per - 1, 0), 0))
    return pl.pallas_call(
        functools.partial(_conv_kernel, ts=ts, d=d),
        grid=(batch, ns),
        in_specs=[prev, cur, _resident(dw_w.shape), _resident((1, d)), _resident((1, d)), _resident((1, d))],
        out_specs=cur,
        out_shape=jax.ShapeDtypeStruct((t, d), BF16),
        scratch_shapes=[pltpu.VMEM((ts + CONV_HALO, d), F32), pltpu.VMEM((ts, d), F32)],
        compiler_params=_params(("parallel", "parallel")),
        name="conv_branch",
    )(u, u, dw_w, dw_b, norm_g, norm_b)


def _to_slabs(ref, val, rows):
    for c in range(SLAB):
        ref[pl.ds(c, rows, stride=SLAB), :] = val[:, c * LANES:(c + 1) * LANES]


def _from_slabs(ref, rows, lead=()):
    return jnp.concatenate([ref[lead + (pl.ds(c, rows, stride=SLAB), slice(None))] for c in range(SLAB)], axis=1)


def _mix_kernel(x_ref, c_ref, o_ref, gate_ref, wc_ref, wa_ref, wo_ref, g2_ref, rw_hi_ref, rw_lo_ref, rb_ref,
                x1_ref, h2s_ref, topi_ref, topw_ref, cnt_ref, *, d, tm):
    y_conv = jnp.dot(c_ref[...], wc_ref[...], preferred_element_type=F32)
    y_attn = jnp.dot(o_ref[...], wa_ref[...], preferred_element_type=F32)
    m = gate_ref[:, :d].astype(F32) * y_conv + gate_ref[:, d:].astype(F32) * y_attn
    x1 = x_ref[...] + jnp.dot(m.astype(BF16), wo_ref[...], preferred_element_type=F32)
    x1_ref[...] = x1
    ms = jnp.mean(x1 * x1, axis=-1, keepdims=True)
    h2 = x1 * lax.rsqrt(ms + EPS) * g2_ref[...]
    h2_hi = h2.astype(BF16)
    _to_slabs(h2s_ref, h2_hi.astype(F32), tm)
    h2_lo = (h2 - h2_hi.astype(F32)).astype(BF16)
    logits = (jnp.dot(h2_hi, rw_hi_ref[...], preferred_element_type=F32)
              + jnp.dot(h2_lo, rw_hi_ref[...], preferred_element_type=F32)
              + jnp.dot(h2_hi, rw_lo_ref[...], preferred_element_type=F32)
              + rb_ref[...])
    lane = lax.broadcasted_iota(jnp.int32, logits.shape, 1)
    work = logits
    top = None
    topi = jnp.zeros(logits.shape, jnp.int32)
    topw = jnp.zeros_like(logits)
    chosen = jnp.zeros_like(logits)
    for k in range(TOP_K):
        mx = jnp.max(work, axis=1, keepdims=True)
        idx = jnp.min(jnp.where(work == mx, lane, LANES), axis=1, keepdims=True)
        sel = lane == idx
        if k == 0:
            top = mx
        topi = jnp.where(lane == k, idx, topi)
        topw = jnp.where(lane == k, jnp.exp(mx - top), topw)
        chosen = jnp.where(sel, 1.0, chosen)
        work = jnp.where(sel, -jnp.inf, work)
    topi_ref[...] = topi
    topw_ref[...] = topw / jnp.sum(topw, axis=1, keepdims=True)
    cnt_ref[0] = jnp.sum(chosen, axis=0, keepdims=True)


def _mix(x2, c, o, gates, wc, wa, wo, g2, rw_hi, rw_lo, rb, *, tm):
    t, d = x2.shape
    row = lambda n: pl.BlockSpec((tm, n), lambda i: (i, 0))
    return pl.pallas_call(
        functools.partial(_mix_kernel, d=d, tm=tm),
        grid=(t // tm,),
        in_specs=[row(d), row(d), row(d), row(2 * d), _resident((d, d)), _resident((d, d)), _resident((d, d)),
                  _resident((1, d)), _resident((d, LANES)), _resident((d, LANES)), _resident((1, LANES))],
        out_specs=[row(d), pl.BlockSpec((tm * SLAB, LANES), lambda i: (i, 0)), row(LANES), row(LANES),
                   pl.BlockSpec((1, 1, LANES), lambda i: (i, 0, 0))],
        out_shape=[jax.ShapeDtypeStruct((t, d), F32), jax.ShapeDtypeStruct((t * SLAB, LANES), F32),
                   jax.ShapeDtypeStruct((t, LANES), jnp.int32), jax.ShapeDtypeStruct((t, LANES), F32),
                   jax.ShapeDtypeStruct((t // tm, 1, LANES), F32)],
        compiler_params=_params(("parallel",)),
        name="mix_router",
    )(x2, c, o, gates, wc, wa, wo, g2, rw_hi, rw_lo, rb)


def _plan_kernel(topi_ref, base_ref, dest_ref, *, tt):
    idx = topi_ref[...]
    lane = lax.broadcasted_iota(jnp.int32, idx.shape, 1)
    onehots = [lane == idx[:, k:k + 1] for k in range(TOP_K)]
    chosen = jnp.zeros(idx.shape, F32)
    for oh in onehots:
        chosen = jnp.where(oh, 1.0, chosen)
    r_iota = lax.broadcasted_iota(jnp.int32, (tt, tt), 0)
    c_iota = lax.broadcasted_iota(jnp.int32, (tt, tt), 1)
    earlier = jnp.where(c_iota < r_iota, 1.0, 0.0).astype(BF16)
    rank = jnp.dot(earlier, chosen.astype(BF16), preferred_element_type=F32)
    pos = rank + base_ref[0]
    dest = jnp.zeros(idx.shape, F32)
    for k, oh in enumerate(onehots):
        dk = jnp.sum(jnp.where(oh, pos, 0.0), axis=1, keepdims=True)
        dest = jnp.where(lane == k, dk, dest)
    dest_ref[...] = dest.T[0:SLAB, :].astype(jnp.int32)


def _plan(topi, base, *, tt):
    t = topi.shape[0]
    nt = t // tt
    return pl.pallas_call(
        functools.partial(_plan_kernel, tt=tt),
        grid=(nt,),
        in_specs=[pl.BlockSpec((tt, LANES), lambda i: (i, 0)), pl.BlockSpec((1, 1, LANES), lambda i: (i, 0, 0))],
        out_specs=pl.BlockSpec((SLAB, tt), lambda i: (i, 0)),
        out_shape=jax.ShapeDtypeStruct((nt * SLAB, tt), jnp.int32),
        compiler_params=_params(("parallel",)),
        name="moe_plan",
    )(topi, base)


def _row_copy(src_ref, src_row, dst_ref, dst_row, sem):
    return pltpu.make_async_copy(src_ref.at[pl.ds(pl.multiple_of(src_row * SLAB, SLAB), SLAB)],
                                 dst_ref.at[pl.ds(pl.multiple_of(dst_row * SLAB, SLAB), SLAB)], sem)


def _dest_load(dest_hbm, dest_smem, isem, tile, slot, tt):
    n = SLAB * tt
    return pltpu.make_async_copy(dest_hbm.at[pl.ds(pl.multiple_of(tile * n, n), n)],
                                 dest_smem.at[pl.ds(pl.multiple_of(slot * n, n), n)], isem)


def _pad_fill_copies(pad_start_ref, pad_cnt_ref, zero_ref, xs_hbm, sem, e, tr):
    cnt = pad_cnt_ref[e]
    out = []
    bit = 1
    while bit < tr:
        first = pad_start_ref[e] + (cnt & (bit - 1))
        copy = pltpu.make_async_copy(zero_ref.at[pl.ds(0, bit * SLAB)],
                                     xs_hbm.at[pl.ds(pl.multiple_of(first * SLAB, SLAB), bit * SLAB)], sem)
        out.append(((cnt & bit) != 0, copy))
        bit *= 2
    return out


def _dispatch_kernel(pad_start_ref, pad_cnt_ref, dest_hbm, h2s_ref, xs_hbm, dest_smem, zero_ref, sem, isem, zsem,
                     *, tt, tr, n_e):
    i = pl.program_id(0)
    slot = i % 2

    @pl.when(i == 0)
    def _():
        _dest_load(dest_hbm, dest_smem, isem, 0, 0, tt).start()

    @pl.when(i == 0)
    def _():
        zero_ref[...] = jnp.zeros_like(zero_ref)
        for e in range(n_e):
            for pred, copy in _pad_fill_copies(pad_start_ref, pad_cnt_ref, zero_ref, xs_hbm, zsem, e, tr):
                pl.when(pred)(copy.start)

    _dest_load(dest_hbm, dest_smem, isem, i, slot, tt).wait()

    @pl.when(i + 1 < pl.num_programs(0))
    def _():
        _dest_load(dest_hbm, dest_smem, isem, i + 1, 1 - slot, tt).start()

    def issue(g, carry):
        toks = [g * ISSUE_GROUP + j for j in range(ISSUE_GROUP)]
        rows = [[dest_smem[(slot * SLAB + k) * tt + tok] for k in range(TOP_K)] for tok in toks]
        for m, (tok, row) in enumerate(zip(toks, rows)):
            for k in range(TOP_K):
                _row_copy(h2s_ref, tok, xs_hbm, row[k], sem).start(priority=(m * TOP_K + k) % 2)
        return carry
    lax.fori_loop(0, tt // ISSUE_GROUP, issue, 0)

    def drain(tok, carry):
        for k in range(TOP_K):
            _row_copy(h2s_ref, 0, xs_hbm, 0, sem).wait()
        return carry
    lax.fori_loop(0, tt, drain, 0, unroll=8)

    @pl.when(i == pl.num_programs(0) - 1)
    def _():
        for e in range(n_e):
            for pred, copy in _pad_fill_copies(pad_start_ref, pad_cnt_ref, zero_ref, xs_hbm, zsem, e, tr):
                pl.when(pred)(copy.wait)


def _dispatch(pad_start, pad_cnt, dest, h2s, *, rows, tt, tr):
    nt = dest.shape[0] // (SLAB * tt)
    n_e = pad_start.shape[0]
    anyspec = pl.BlockSpec(memory_space=pl.ANY)
    return pl.pallas_call(
        functools.partial(_dispatch_kernel, tt=tt, tr=tr, n_e=n_e),
        grid_spec=pltpu.PrefetchScalarGridSpec(
            num_scalar_prefetch=2, grid=(nt,),
            in_specs=[anyspec, pl.BlockSpec((tt * SLAB, LANES), lambda i, ps, pc: (i, 0))], out_specs=anyspec,
            scratch_shapes=[pltpu.SMEM((2 * SLAB * tt,), jnp.int32), pltpu.VMEM((tr // 2 * SLAB, LANES), F32),
                            pltpu.SemaphoreType.DMA, pltpu.SemaphoreType.DMA, pltpu.SemaphoreType.DMA]),
        out_shape=jax.ShapeDtypeStruct((rows * SLAB, LANES), F32),
        compiler_params=_params(("arbitrary",)),
        name="moe_dispatch",
    )(pad_start, pad_cnt, dest, h2s)


def _ffn_kernel(te_ref, nu_ref, xs_ref, w1_ref, b1_ref, w2_ref, b2_ref, o_ref, w1b_ref, w2b_ref, *, tr, dff):
    r = pl.program_id(0)

    @pl.when(jnp.logical_or(r == 0, te_ref[r] != te_ref[jnp.maximum(r - 1, 0)]))
    def _():
        w1b_ref[...] = w1_ref[0].astype(BF16)
        w2b_ref[...] = w2_ref[0].astype(BF16)

    @pl.when(r < nu_ref[0])
    def _():
        xs = _from_slabs(xs_ref, tr).astype(BF16)
        hid = jnp.dot(xs, w1b_ref[...], preferred_element_type=F32) + b1_ref[0]
        glu = jnp.minimum(hid[:, :dff], SWIGLU_LIMIT)
        lin = jnp.clip(hid[:, dff:], -SWIGLU_LIMIT, SWIGLU_LIMIT)
        act = glu * _sigmoid(SWIGLU_ALPHA * glu) * (lin + 1.0)
        out = jnp.dot(act.astype(BF16), w2b_ref[...], preferred_element_type=F32) + b2_ref[0]
        _to_slabs(o_ref, out, tr)


def _ffn(tile_expert, n_used, xs, w1, b1, w2, b2, *, tr):
    n_e, d, dff2 = w1.shape
    n_tiles = tile_expert.shape[0]
    rows = lambda r, te, nu: (jnp.minimum(r, nu[0] - 1), 0)
    per_e = lambda a, b: pl.BlockSpec((1, a, b), lambda r, te, nu: (te[r], 0, 0))
    return pl.pallas_call(
        functools.partial(_ffn_kernel, tr=tr, dff=dff2 // 2),
        grid_spec=pltpu.PrefetchScalarGridSpec(
            num_scalar_prefetch=2, grid=(n_tiles,),
            in_specs=[pl.BlockSpec((tr * SLAB, LANES), rows), per_e(d, dff2), per_e(1, dff2), per_e(dff2 // 2, d),
                      per_e(1, d)],
            out_specs=pl.BlockSpec((tr * SLAB, LANES), rows),
            scratch_shapes=[pltpu.VMEM((d, dff2), BF16), pltpu.VMEM((dff2 // 2, d), BF16)]),
        out_shape=jax.ShapeDtypeStruct(xs.shape, F32),
        compiler_params=_params(("arbitrary",)),
        name="moe_experts",
    )(tile_expert, n_used, xs, w1, b1, w2, b2)


def _combine_kernel(dest_hbm, ys_hbm, x1_ref, topw_ref, fg_ref, o_ref, dest_smem, buf_ref, sems, isem, *, tt):
    i = pl.program_id(0)
    n = pl.num_programs(0)

    def gather(tile):
        slot = tile % 2
        _dest_load(dest_hbm, dest_smem, isem, tile, slot, tt).wait()

        @pl.when(tile + 1 < n)
        def _():
            _dest_load(dest_hbm, dest_smem, isem, tile + 1, 1 - slot, tt).start()

        def issue(g, carry):
            toks = [g * ISSUE_GROUP + j for j in range(ISSUE_GROUP)]
            rows = [[dest_smem[(slot * SLAB + k) * tt + tok] for k in range(TOP_K)] for tok in toks]
            for m, (tok, row) in enumerate(zip(toks, rows)):
                for k in range(TOP_K):
                    _row_copy(ys_hbm, row[k], buf_ref.at[slot, k], tok, sems.at[slot]).start(
                        priority=(m * TOP_K + k) % 2)
            return carry
        lax.fori_loop(0, tt // ISSUE_GROUP, issue, 0)

    @pl.when(i == 0)
    def _():
        _dest_load(dest_hbm, dest_smem, isem, 0, 0, tt).start()
        gather(0)

    @pl.when(i + 1 < n)
    def _():
        gather(i + 1)

    slot = i % 2

    def drain(tok, carry):
        for k in range(TOP_K):
            _row_copy(ys_hbm, 0, buf_ref.at[slot, k], 0, sems.at[slot]).wait()
        return carry
    lax.fori_loop(0, tt, drain, 0, unroll=8)

    y = x1_ref[...]
    w = topw_ref[...]
    for k in range(TOP_K):
        y = y + w[:, k:k + 1] * _from_slabs(buf_ref, tt, lead=(slot, k))
    ms = jnp.mean(y * y, axis=-1, keepdims=True)
    o_ref[...] = y * lax.rsqrt(ms + EPS) * fg_ref[...]


def _combine(dest, ys, x1, topw, fg, *, tt):
    t, d = x1.shape
    row = lambda n: pl.BlockSpec((tt, n), lambda i: (i, 0))
    anyspec = pl.BlockSpec(memory_space=pl.ANY)
    return pl.pallas_call(
        functools.partial(_combine_kernel, tt=tt),
        grid=(t // tt,),
        in_specs=[anyspec, anyspec, row(d), row(LANES), _resident((1, d))],
        out_specs=row(d),
        out_shape=jax.ShapeDtypeStruct((t, d), F32),
        scratch_shapes=[pltpu.SMEM((2 * SLAB * tt,), jnp.int32), pltpu.VMEM((2, TOP_K, tt * SLAB, LANES), F32),
                        pltpu.SemaphoreType.DMA((2,)), pltpu.SemaphoreType.DMA],
        compiler_params=_params(("arbitrary",)),
        name="moe_combine",
    )(dest, ys, x1, topw, fg)


def _moe(x1, h2s, topi, topw, counts, w1, b1, w2, b2, fg, *, tt, tr):
    t = x1.shape[0]
    n_e = w1.shape[0]
    cnt = counts[:, 0, :n_e].astype(jnp.int32)
    nt = cnt.shape[0]
    total = jnp.sum(cnt, axis=0)
    padded = (total + tr - 1) // tr * tr
    e_ids = jnp.arange(n_e)
    ends = jnp.sum(jnp.where(e_ids[:, None] <= e_ids[None, :], padded[:, None], 0), axis=0)
    starts = ends - padded
    t_ids = jnp.arange(nt)
    before = jnp.sum(jnp.where((t_ids[:, None] < t_ids[None, :])[:, :, None], cnt[:, None, :], 0), axis=0)
    base = starts[None, :] + before
    base = jnp.pad(base.astype(F32), ((0, 0), (0, LANES - n_e)))[:, None, :]
    n_tiles = (t * TOP_K) // tr + n_e
    n_used = (ends[-1] // tr).astype(jnp.int32)
    r = jnp.minimum(jnp.arange(n_tiles, dtype=jnp.int32), n_used - 1)
    tile_expert = jnp.sum((ends // tr)[None, :] <= r[:, None], axis=1).astype(jnp.int32)

    dest = _plan(topi, base, tt=tt).reshape(-1)
    xs = _dispatch(starts + total, padded - total, dest, h2s, rows=n_tiles * tr, tt=tt, tr=tr)
    ys = _ffn(tile_expert, n_used.reshape(1), xs, w1, b1, w2, b2, tr=tr)
    return _combine(dest, ys, x1, topw, fg, tt=tt)


def _tile(n, want):
    want = min(want, n)
    assert n % want == 0, (n, want)
    return want


def kernel(x, norm1_g, w_in, conv_dw_w, conv_dw_b, conv_norm_g, conv_norm_b, w_conv_out, w_attn_out, gate_b,
           w_out, norm2_g, router_w, router_b, expert_w1, expert_b1, expert_w2, expert_b2, final_norm_g):
    batch, seq, d = x.shape
    depth = w_in.shape[0]
    t = batch * seq
    assert d == SLAB * LANES and N_EXPERTS <= LANES
    assert depth == 1, "the final rms_norm is fused into the combine kernel of the only layer"
    row2 = lambda a: a.reshape(1, -1)
    l = 0

    tb = _tile(seq, 256)
    tt = _tile(t, 512)
    x2 = x.reshape(t, d)
    u, q, k, v, gates = _in_proj(x2, row2(norm1_g[l]), w_in[l].astype(BF16), row2(gate_b[l]), tm=_tile(t, 512))
    o = _attention(q, k, v, batch=batch, seq=seq, tb=tb, heads=ATTN_HEADS_PER_STEP)
    c = _conv_branch(u, conv_dw_w[l], row2(conv_dw_b[l]), row2(conv_norm_g[l]), row2(conv_norm_b[l]),
                     batch=batch, seq=seq, ts=_tile(seq, 256))
    rw = jnp.pad(router_w[l], ((0, 0), (0, LANES - N_EXPERTS)))
    rw_hi = rw.astype(BF16)
    rw_lo = (rw - rw_hi.astype(F32)).astype(BF16)
    rb = jnp.pad(router_b[l].astype(F32), (0, LANES - N_EXPERTS), constant_values=-1e30).reshape(1, LANES)
    x1, h2s, topi, topw, counts = _mix(x2, c, o, gates, w_conv_out[l].astype(BF16), w_attn_out[l].astype(BF16),
                                       w_out[l].astype(BF16), row2(norm2_g[l]), rw_hi, rw_lo, rb, tm=tt)
    out = _moe(x1, h2s, topi, topw, counts, expert_w1[l], expert_b1[l][:, None, :],
               expert_w2[l], expert_b2[l][:, None, :], row2(final_norm_g), tt=tt,
               tr=_tile(t * TOP_K, 512))
    return out.reshape(batch, seq, d)
```

```python
import functools

import jax
import jax.numpy as jnp
from jax import lax
from jax.experimental import pallas as pl
from jax.experimental.pallas import tpu as pltpu

F32 = jnp.float32
BF16 = jnp.bfloat16

HEAD_DIM = 64
CONV_WIDTH = 31
N_EXPERTS = 32
TOP_K = 4
SWIGLU_LIMIT = 7.0
SWIGLU_ALPHA = 1.702
EPS = 1e-6

LANES = 128
SLAB = 8
ATTN_HEADS_PER_STEP = 4
ISSUE_GROUP = 8
CONV_HALO = 32
CONV_ROWS = 128
VMEM_LIMIT = 56 * 1024 * 1024

LOG2_E = 1.4426950408889634
SKIP_THRESHOLD = 150.0 * (1.0 + 2.0 ** -7)
SIGN_BIT = 0x80000000


def _params(sem):
    return pltpu.CompilerParams(dimension_semantics=sem, vmem_limit_bytes=VMEM_LIMIT)


def _resident(shape):
    return pl.BlockSpec(shape, lambda *_: (0,) * len(shape), pipeline_mode=pl.Buffered(1))


def _sigmoid(x):
    return 1.0 / (1.0 + jnp.exp(-x))


def _inproj_kernel(x_ref, g_ref, w_ref, gb_ref, u_ref, q_ref, k_ref, v_ref, gate_ref, *, d):
    x = x_ref[...]
    ms = jnp.mean(x * x, axis=-1, keepdims=True)
    h = (x * lax.rsqrt(ms + EPS) * g_ref[...]).astype(BF16)

    def proj(c):
        return jnp.dot(h, w_ref[:, c * d:(c + 1) * d], preferred_element_type=F32)

    u_ref[...] = (proj(0) * _sigmoid(proj(1))).astype(BF16)
    q_ref[...] = (proj(2) * (HEAD_DIM ** -0.5 * LOG2_E)).astype(BF16)
    k_ref[...] = proj(3).astype(BF16)
    v_ref[...] = proj(4).astype(BF16)
    gate_ref[:, :d] = _sigmoid(proj(5) + gb_ref[:, :d]).astype(BF16)
    gate_ref[:, d:] = _sigmoid(proj(6) + gb_ref[:, d:]).astype(BF16)


def _in_proj(x2, norm_g, w_in, gate_b, *, tm):
    t, d = x2.shape
    row = lambda n: pl.BlockSpec((tm, n), lambda i: (i, 0))
    out = lambda n: jax.ShapeDtypeStruct((t, n), BF16)
    return pl.pallas_call(
        functools.partial(_inproj_kernel, d=d),
        grid=(t // tm,),
        in_specs=[row(d), _resident((1, d)), _resident(w_in.shape), _resident((1, 2 * d))],
        out_specs=[row(d), row(d), row(d), row(d), row(2 * d)],
        out_shape=[out(d), out(d), out(d), out(d), out(2 * d)],
        compiler_params=_params(("parallel",)),
        name="in_proj",
    )(x2, norm_g, w_in, gate_b)


def _attn_kernel(q_ref, k_ref, v_ref, o_ref, *, tb, heads):
    i = pl.program_id(2)
    r_iota = lax.broadcasted_iota(jnp.int32, (tb, tb), 0)
    c_iota = lax.broadcasted_iota(jnp.int32, (tb, tb), 1)
    upper = jnp.where(r_iota > c_iota, 1.0, 0.0).astype(BF16)
    causal = c_iota < r_iota

    first_head = lax.broadcasted_iota(jnp.int32, (tb, LANES), 1) < HEAD_DIM
    zeros = jnp.zeros((tb, LANES), BF16)

    def split_heads(x2):
        return jnp.concatenate([jnp.where(first_head, x2, zeros), jnp.where(first_head, zeros, x2)], axis=0)

    def pair_block(p, j, carry, acc, diagonal):
        lanes = slice(p * LANES, (p + 1) * LANES)
        start = pl.multiple_of(j * tb, tb)
        z = lax.dot_general(q_ref[:, lanes], split_heads(k_ref[pl.ds(start, tb), lanes]),
                            (((1,), (1,)), ((), ())), preferred_element_type=F32)
        neg_abs = pltpu.bitcast(pltpu.bitcast(z, jnp.uint32) | jnp.uint32(SIGN_BIT), F32)
        l = jnp.log2(1.0 + jnp.exp2(neg_abs))
        sp = jnp.maximum(z, 0.0) + l
        logsig = z - sp
        if diagonal:
            sp = jnp.where(causal2, sp, 0.0)
        spb = sp.astype(BF16)
        prefix = jnp.dot(jnp.concatenate([spb[:, :tb], spb[:, tb:]], axis=0), upper,
                         preferred_element_type=F32)
        mass = (prefix[0:tb, 0:1] + sp[:, 0:1], prefix[tb:, 0:1] + sp[:, tb:tb + 1])
        later = jnp.concatenate([prefix[0:tb] + carry[0], prefix[tb:] + carry[1]], axis=1)
        a = jnp.exp2(logsig - later)
        if diagonal:
            a = jnp.where(causal2, a, 0.0)
        acc = acc + jnp.dot(a.astype(BF16), split_heads(v_ref[pl.ds(start, tb), lanes]),
                            preferred_element_type=F32)
        return (carry[0] + mass[0], carry[1] + mass[1]), acc

    pairs = heads * HEAD_DIM // LANES
    causal2 = jnp.concatenate([causal, causal], axis=1)

    def blocks(j, carries, accs, diagonal):
        new = [pair_block(p, j, carries[p], accs[p], diagonal) for p in range(pairs)]
        return tuple(c for c, _ in new), tuple(a for _, a in new)

    zero_carry = (jnp.zeros((tb, 1), F32), jnp.zeros((tb, 1), F32))

    def least_mass(carries):
        return jnp.min(functools.reduce(jnp.minimum, [c for pair in carries for c in pair]))

    def leading_blocks(with_previous):
        carries, accs = blocks(i, (zero_carry,) * pairs, (jnp.zeros((tb, LANES), F32),) * pairs, True)
        if with_previous:
            carries, accs = blocks(i - 1, carries, accs, False)
        return least_mass(carries), carries, accs

    least, carries, accs = lax.cond(i > 0, lambda: leading_blocks(True), lambda: leading_blocks(False))

    def cond(state):
        j, least, _, _ = state
        return jnp.logical_and(j >= 0, least < SKIP_THRESHOLD)

    def body(state):
        j, _, carries, accs = state
        carries, accs = blocks(j, carries, accs, False)
        return j - 1, least_mass(carries), carries, accs

    _, _, _, accs = lax.while_loop(cond, body, (i - 2, least, carries, accs))
    for p in range(pairs):
        o_ref[:, p * LANES:(p + 1) * LANES] = accs[p].astype(BF16)


def _attention(q, k, v, *, batch, seq, tb, heads):
    t, d = q.shape
    nq = seq // tb
    width = heads * HEAD_DIM
    qspec = pl.BlockSpec((tb, width), lambda b, hp, i: (b * nq + i, hp))
    kvspec = pl.BlockSpec((seq, width), lambda b, hp, i: (b, hp))
    return pl.pallas_call(
        functools.partial(_attn_kernel, tb=tb, heads=heads),
        grid=(batch, d // width, nq),
        in_specs=[qspec, kvspec, kvspec],
        out_specs=qspec,
        out_shape=jax.ShapeDtypeStruct((t, d), BF16),
        compiler_params=_params(("parallel", "parallel", "arbitrary")),
        name="attention",
    )(q, k, v)


def _conv_kernel(prev_ref, cur_ref, w_ref, b_ref, g_ref, nb_ref, o_ref, win_ref, conv_ref, *, ts, d):
    i = pl.program_id(1)
    prev = prev_ref[...].astype(F32)
    win_ref[0:CONV_HALO, :] = jnp.where(i == 0, 0.0, prev)
    win_ref[CONV_HALO:, :] = cur_ref[...].astype(F32)
    base = CONV_HALO - (CONV_WIDTH - 1)
    rows = CONV_ROWS
    span = rows + CONV_HALO
    for c in range(d // LANES):
        lanes = slice(c * LANES, (c + 1) * LANES)
        for r0 in range(0, ts, rows):
            window = win_ref[r0:r0 + span, lanes]
            acc = jnp.zeros((rows, LANES), F32)
            for shift in range(SLAB):
                shifted = pltpu.roll(window, (span - shift) % span, axis=0) if shift else window
                for w in range(CONV_WIDTH):
                    if (base + w) % SLAB == shift:
                        lead = (base + w) // SLAB * SLAB
                        acc = acc + shifted[lead:lead + rows] * w_ref[w:w + 1, lanes]
            conv_ref[r0:r0 + rows, lanes] = acc + b_ref[:, lanes]
    y = conv_ref[...]
    mu = jnp.mean(y, axis=-1, keepdims=True)
    yc = y - mu
    var = jnp.mean(yc * yc, axis=-1, keepdims=True)
    yn = yc * lax.rsqrt(var + EPS) * g_ref[...] + nb_ref[...]
    o_ref[...] = (yn * _sigmoid(yn)).astype(BF16)


def _conv_branch(u, dw_w, dw_b, norm_g, norm_b, *, batch, seq, ts):
    t, d = u.shape
    ns = seq // ts
    per = ts // CONV_HALO
    cur = pl.BlockSpec((ts, d), lambda b, i: (b * ns + i, 0))
    prev = pl.BlockSpec((CONV_HALO, d), lambda b, i: (jnp.maximum((b * ns + i) * per - 1, 0), 0))
    return pl.pallas_call(
        functools.partial(_conv_kernel, ts=ts, d=d),
        grid=(batch, ns),
        in_specs=[prev, cur, _resident(dw_w.shape), _resident((1, d)), _resident((1, d)), _resident((1, d))],
        out_specs=cur,
        out_shape=jax.ShapeDtypeStruct((t, d), BF16),
        scratch_shapes=[pltpu.VMEM((ts + CONV_HALO, d), F32), pltpu.VMEM((ts, d), F32)],
        compiler_params=_params(("parallel", "parallel")),
        name="conv_branch",
    )(u, u, dw_w, dw_b, norm_g, norm_b)


def _to_slabs(ref, val, rows):
    for c in range(SLAB):
        ref[pl.ds(c, rows, stride=SLAB), :] = val[:, c * LANES:(c + 1) * LANES]


def _from_slabs(ref, rows, lead=()):
    return jnp.concatenate([ref[lead + (pl.ds(c, rows, stride=SLAB), slice(None))] for c in range(SLAB)], axis=1)


def _mix_kernel(x_ref, c_ref, o_ref, gate_ref, wc_ref, wa_ref, wo_ref, g2_ref, rw_hi_ref, rw_lo_ref, rb_ref,
                x1_ref, h2s_ref, topi_ref, topw_ref, cnt_ref, *, d, tm):
    y_conv = jnp.dot(c_ref[...], wc_ref[...], preferred_element_type=F32)
    y_attn = jnp.dot(o_ref[...], wa_ref[...], preferred_element_type=F32)
    m = gate_ref[:, :d].astype(F32) * y_conv + gate_ref[:, d:].astype(F32) * y_attn
    x1 = x_ref[...] + jnp.dot(m.astype(BF16), wo_ref[...], preferred_element_type=F32)
    x1_ref[...] = x1
    ms = jnp.mean(x1 * x1, axis=-1, keepdims=True)
    h2 = x1 * lax.rsqrt(ms + EPS) * g2_ref[...]
    h2_hi = h2.astype(BF16)
    _to_slabs(h2s_ref, h2_hi.astype(F32), tm)
    h2_lo = (h2 - h2_hi.astype(F32)).astype(BF16)
    logits = (jnp.dot(h2_hi, rw_hi_ref[...], preferred_element_type=F32)
              + jnp.dot(h2_lo, rw_hi_ref[...], preferred_element_type=F32)
              + jnp.dot(h2_hi, rw_lo_ref[...], preferred_element_type=F32)
              + rb_ref[...])
    lane = lax.broadcasted_iota(jnp.int32, logits.shape, 1)
    work = logits
    top = None
    topi = jnp.zeros(logits.shape, jnp.int32)
    topw = jnp.zeros_like(logits)
    chosen = jnp.zeros_like(logits)
    for k in range(TOP_K):
        mx = jnp.max(work, axis=1, keepdims=True)
        idx = jnp.min(jnp.where(work == mx, lane, LANES), axis=1, keepdims=True)
        sel = lane == idx
        if k == 0:
            top = mx
        topi = jnp.where(lane == k, idx, topi)
        topw = jnp.where(lane == k, jnp.exp(mx - top), topw)
        chosen = jnp.where(sel, 1.0, chosen)
        work = jnp.where(sel, -jnp.inf, work)
    topi_ref[...] = topi
    topw_ref[...] = topw / jnp.sum(topw, axis=1, keepdims=True)
    cnt_ref[0] = jnp.sum(chosen, axis=0, keepdims=True)


def _mix(x2, c, o, gates, wc, wa, wo, g2, rw_hi, rw_lo, rb, *, tm):
    t, d = x2.shape
    row = lambda n: pl.BlockSpec((tm, n), lambda i: (i, 0))
    return pl.pallas_call(
        functools.partial(_mix_kernel, d=d, tm=tm),
        grid=(t // tm,),
        in_specs=[row(d), row(d), row(d), row(2 * d), _resident((d, d)), _resident((d, d)), _resident((d, d)),
                  _resident((1, d)), _resident((d, LANES)), _resident((d, LANES)), _resident((1, LANES))],
        out_specs=[row(d), pl.BlockSpec((tm * SLAB, LANES), lambda i: (i, 0)), row(LANES), row(LANES),
                   pl.BlockSpec((1, 1, LANES), lambda i: (i, 0, 0))],
        out_shape=[jax.ShapeDtypeStruct((t, d), F32), jax.ShapeDtypeStruct((t * SLAB, LANES), F32),
                   jax.ShapeDtypeStruct((t, LANES), jnp.int32), jax.ShapeDtypeStruct((t, LANES), F32),
                   jax.ShapeDtypeStruct((t // tm, 1, LANES), F32)],
        compiler_params=_params(("parallel",)),
        name="mix_router",
    )(x2, c, o, gates, wc, wa, wo, g2, rw_hi, rw_lo, rb)


def _plan_kernel(topi_ref, base_ref, dest_ref, *, tt):
    idx = topi_ref[...]
    lane = lax.broadcasted_iota(jnp.int32, idx.shape, 1)
    onehots = [lane == idx[:, k:k + 1] for k in range(TOP_K)]
    chosen = jnp.zeros(idx.shape, F32)
    for oh in onehots:
        chosen = jnp.where(oh, 1.0, chosen)
    r_iota = lax.broadcasted_iota(jnp.int32, (tt, tt), 0)
    c_iota = lax.broadcasted_iota(jnp.int32, (tt, tt), 1)
    earlier = jnp.where(c_iota < r_iota, 1.0, 0.0).astype(BF16)
    rank = jnp.dot(earlier, chosen.astype(BF16), preferred_element_type=F32)
    pos = rank + base_ref[0]
    dest = jnp.zeros(idx.shape, F32)
    for k, oh in enumerate(onehots):
        dk = jnp.sum(jnp.where(oh, pos, 0.0), axis=1, keepdims=True)
        dest = jnp.where(lane == k, dk, dest)
    dest_ref[...] = dest.T[0:SLAB, :].astype(jnp.int32)


def _plan(topi, base, *, tt):
    t = topi.shape[0]
    nt = t // tt
    return pl.pallas_call(
        functools.partial(_plan_kernel, tt=tt),
        grid=(nt,),
        in_specs=[pl.BlockSpec((tt, LANES), lambda i: (i, 0)), pl.BlockSpec((1, 1, LANES), lambda i: (i, 0, 0))],
        out_specs=pl.BlockSpec((SLAB, tt), lambda i: (i, 0)),
        out_shape=jax.ShapeDtypeStruct((nt * SLAB, tt), jnp.int32),
        compiler_params=_params(("parallel",)),
        name="moe_plan",
    )(topi, base)


def _row_copy(src_ref, src_row, dst_ref, dst_row, sem):
    return pltpu.make_async_copy(src_ref.at[pl.ds(pl.multiple_of(src_row * SLAB, SLAB), SLAB)],
                                 dst_ref.at[pl.ds(pl.multiple_of(dst_row * SLAB, SLAB), SLAB)], sem)


def _dest_load(dest_hbm, dest_smem, isem, tile, slot, tt):
    n = SLAB * tt
    return pltpu.make_async_copy(dest_hbm.at[pl.ds(pl.multiple_of(tile * n, n), n)],
                                 dest_smem.at[pl.ds(pl.multiple_of(slot * n, n), n)], isem)


def _pad_fill_copies(pad_start_ref, pad_cnt_ref, zero_ref, xs_hbm, sem, e, tr):
    cnt = pad_cnt_ref[e]
    out = []
    bit = 1
    while bit < tr:
        first = pad_start_ref[e] + (cnt & (bit - 1))
        copy = pltpu.make_async_copy(zero_ref.at[pl.ds(0, bit * SLAB)],
                                     xs_hbm.at[pl.ds(pl.multiple_of(first * SLAB, SLAB), bit * SLAB)], sem)
        out.append(((cnt & bit) != 0, copy))
        bit *= 2
    return out


def _dispatch_kernel(pad_start_ref, pad_cnt_ref, dest_hbm, h2s_ref, xs_hbm, dest_smem, zero_ref, sem, isem, zsem,
                     *, tt, tr, n_e):
    i = pl.program_id(0)
    slot = i % 2

    @pl.when(i == 0)
    def _():
        _dest_load(dest_hbm, dest_smem, isem, 0, 0, tt).start()

    @pl.when(i == 0)
    def _():
        zero_ref[...] = jnp.zeros_like(zero_ref)
        for e in range(n_e):
            for pred, copy in _pad_fill_copies(pad_start_ref, pad_cnt_ref, zero_ref, xs_hbm, zsem, e, tr):
                pl.when(pred)(copy.start)

    _dest_load(dest_hbm, dest_smem, isem, i, slot, tt).wait()

    @pl.when(i + 1 < pl.num_programs(0))
    def _():
        _dest_load(dest_hbm, dest_smem, isem, i + 1, 1 - slot, tt).start()

    def issue(g, carry):
        toks = [g * ISSUE_GROUP + j for j in range(ISSUE_GROUP)]
        rows = [[dest_smem[(slot * SLAB + k) * tt + tok] for k in range(TOP_K)] for tok in toks]
        for m, (tok, row) in enumerate(zip(toks, rows)):
            for k in range(TOP_K):
                _row_copy(h2s_ref, tok, xs_hbm, row[k], sem).start(priority=(m * TOP_K + k) % 2)
        return carry
    lax.fori_loop(0, tt // ISSUE_GROUP, issue, 0)

    def drain(tok, carry):
        for k in range(TOP_K):
            _row_copy(h2s_ref, 0, xs_hbm, 0, sem).wait()
        return carry
    lax.fori_loop(0, tt, drain, 0, unroll=8)

    @pl.when(i == pl.num_programs(0) - 1)
    def _():
        for e in range(n_e):
            for pred, copy in _pad_fill_copies(pad_start_ref, pad_cnt_ref, zero_ref, xs_hbm, zsem, e, tr):
                pl.when(pred)(copy.wait)


def _dispatch(pad_start, pad_cnt, dest, h2s, *, rows, tt, tr):
    nt = dest.shape[0] // (SLAB * tt)
    n_e = pad_start.shape[0]
    anyspec = pl.BlockSpec(memory_space=pl.ANY)
    return pl.pallas_call(
        functools.partial(_dispatch_kernel, tt=tt, tr=tr, n_e=n_e),
        grid_spec=pltpu.PrefetchScalarGridSpec(
            num_scalar_prefetch=2, grid=(nt,),
            in_specs=[anyspec, pl.BlockSpec((tt * SLAB, LANES), lambda i, ps, pc: (i, 0))], out_specs=anyspec,
            scratch_shapes=[pltpu.SMEM((2 * SLAB * tt,), jnp.int32), pltpu.VMEM((tr // 2 * SLAB, LANES), F32),
                            pltpu.SemaphoreType.DMA, pltpu.SemaphoreType.DMA, pltpu.SemaphoreType.DMA]),
        out_shape=jax.ShapeDtypeStruct((rows * SLAB, LANES), F32),
        compiler_params=_params(("arbitrary",)),
        name="moe_dispatch",
    )(pad_start, pad_cnt, dest, h2s)


def _ffn_kernel(te_ref, nu_ref, xs_ref, w1_ref, b1_ref, w2_ref, b2_ref, o_ref, w1b_ref, w2b_ref, *, tr, dff):
    r = pl.program_id(0)

    @pl.when(jnp.logical_or(r == 0, te_ref[r] != te_ref[jnp.maximum(r - 1, 0)]))
    def _():
        w1b_ref[...] = w1_ref[0].astype(BF16)
        w2b_ref[...] = w2_ref[0].astype(BF16)

    @pl.when(r < nu_ref[0])
    def _():
        xs = _from_slabs(xs_ref, tr).astype(BF16)
        hid = jnp.dot(xs, w1b_ref[...], preferred_element_type=F32) + b1_ref[0]
        glu = jnp.minimum(hid[:, :dff], SWIGLU_LIMIT)
        lin = jnp.clip(hid[:, dff:], -SWIGLU_LIMIT, SWIGLU_LIMIT)
        act = glu * _sigmoid(SWIGLU_ALPHA * glu) * (lin + 1.0)
        out = jnp.dot(act.astype(BF16), w2b_ref[...], preferred_element_type=F32) + b2_ref[0]
        _to_slabs(o_ref, out, tr)


def _ffn(tile_expert, n_used, xs, w1, b1, w2, b2, *, tr):
    n_e, d, dff2 = w1.shape
    n_tiles = tile_expert.shape[0]
    rows = lambda r, te, nu: (jnp.minimum(r, nu[0] - 1), 0)
    per_e = lambda a, b: pl.BlockSpec((1, a, b), lambda r, te, nu: (te[r], 0, 0))
    return pl.pallas_call(
        functools.partial(_ffn_kernel, tr=tr, dff=dff2 // 2),
        grid_spec=pltpu.PrefetchScalarGridSpec(
            num_scalar_prefetch=2, grid=(n_tiles,),
            in_specs=[pl.BlockSpec((tr * SLAB, LANES), rows), per_e(d, dff2), per_e(1, dff2), per_e(dff2 // 2, d),
                      per_e(1, d)],
            out_specs=pl.BlockSpec((tr * SLAB, LANES), rows),
            scratch_shapes=[pltpu.VMEM((d, dff2), BF16), pltpu.VMEM((dff2 // 2, d), BF16)]),
        out_shape=jax.ShapeDtypeStruct(xs.shape, F32),
        compiler_params=_params(("arbitrary",)),
        name="moe_experts",
    )(tile_expert, n_used, xs, w1, b1, w2, b2)


def _combine_kernel(dest_hbm, ys_hbm, x1_ref, topw_ref, fg_ref, o_ref, dest_smem, buf_ref, sems, isem, *, tt):
    i = pl.program_id(0)
    n = pl.num_programs(0)

    def gather(tile):
        slot = tile % 2
        _dest_load(dest_hbm, dest_smem, isem, tile, slot, tt).wait()

        @pl.when(tile + 1 < n)
        def _():
            _dest_load(dest_hbm, dest_smem, isem, tile + 1, 1 - slot, tt).start()

        def issue(g, carry):
            toks = [g * ISSUE_GROUP + j for j in range(ISSUE_GROUP)]
            rows = [[dest_smem[(slot * SLAB + k) * tt + tok] for k in range(TOP_K)] for tok in toks]
            for m, (tok, row) in enumerate(zip(toks, rows)):
                for k in range(TOP_K):
                    _row_copy(ys_hbm, row[k], buf_ref.at[slot, k], tok, sems.at[slot]).start(
                        priority=(m * TOP_K + k) % 2)
            return carry
        lax.fori_loop(0, tt // ISSUE_GROUP, issue, 0)

    @pl.when(i == 0)
    def _():
        _dest_load(dest_hbm, dest_smem, isem, 0, 0, tt).start()
        gather(0)

    @pl.when(i + 1 < n)
    def _():
        gather(i + 1)

    slot = i % 2

    def drain(tok, carry):
        for k in range(TOP_K):
            _row_copy(ys_hbm, 0, buf_ref.at[slot, k], 0, sems.at[slot]).wait()
        return carry
    lax.fori_loop(0, tt, drain, 0, unroll=8)

    y = x1_ref[...]
    w = topw_ref[...]
    for k in range(TOP_K):
        y = y + w[:, k:k + 1] * _from_slabs(buf_ref, tt, lead=(slot, k))
    ms = jnp.mean(y * y, axis=-1, keepdims=True)
    o_ref[...] = y * lax.rsqrt(ms + EPS) * fg_ref[...]


def _combine(dest, ys, x1, topw, fg, *, tt):
    t, d = x1.shape
    row = lambda n: pl.BlockSpec((tt, n), lambda i: (i, 0))
    anyspec = pl.BlockSpec(memory_space=pl.ANY)
    return pl.pallas_call(
        functools.partial(_combine_kernel, tt=tt),
        grid=(t // tt,),
        in_specs=[anyspec, anyspec, row(d), row(LANES), _resident((1, d))],
        out_specs=row(d),
        out_shape=jax.ShapeDtypeStruct((t, d), F32),
        scratch_shapes=[pltpu.SMEM((2 * SLAB * tt,), jnp.int32), pltpu.VMEM((2, TOP_K, tt * SLAB, LANES), F32),
                        pltpu.SemaphoreType.DMA((2,)), pltpu.SemaphoreType.DMA],
        compiler_params=_params(("arbitrary",)),
        name="moe_combine",
    )(dest, ys, x1, topw, fg)


def _moe(x1, h2s, topi, topw, counts, w1, b1, w2, b2, fg, *, tt, tr):
    t = x1.shape[0]
    n_e = w1.shape[0]
    cnt = counts[:, 0, :n_e].astype(jnp.int32)
    nt = cnt.shape[0]
    total = jnp.sum(cnt, axis=0)
    padded = (total + tr - 1) // tr * tr
    e_ids = jnp.arange(n_e)
    ends = jnp.sum(jnp.where(e_ids[:, None] <= e_ids[None, :], padded[:, None], 0), axis=0)
    starts = ends - padded
    t_ids = jnp.arange(nt)
    before = jnp.sum(jnp.where((t_ids[:, None] < t_ids[None, :])[:, :, None], cnt[:, None, :], 0), axis=0)
    base = starts[None, :] + before
    base = jnp.pad(base.astype(F32), ((0, 0), (0, LANES - n_e)))[:, None, :]
    n_tiles = (t * TOP_K) // tr + n_e
    n_used = (ends[-1] // tr).astype(jnp.int32)
    r = jnp.minimum(jnp.arange(n_tiles, dtype=jnp.int32), n_used - 1)
    tile_expert = jnp.sum((ends // tr)[None, :] <= r[:, None], axis=1).astype(jnp.int32)

    dest = _plan(topi, base, tt=tt).reshape(-1)
    xs = _dispatch(starts + total, padded - total, dest, h2s, rows=n_tiles * tr, tt=tt, tr=tr)
    ys = _ffn(tile_expert, n_used.reshape(1), xs, w1, b1, w2, b2, tr=tr)
    return _combine(dest, ys, x1, topw, fg, tt=tt)


def _tile(n, want):
    want = min(want, n)
    assert n % want == 0, (n, want)
    return want


def kernel(x, norm1_g, w_in, conv_dw_w, conv_dw_b, conv_norm_g, conv_norm_b, w_conv_out, w_attn_out, gate_b,
           w_out, norm2_g, router_w, router_b, expert_w1, expert_b1, expert_w2, expert_b2, final_norm_g):
    batch, seq, d = x.shape
    depth = w_in.shape[0]
    t = batch * seq
    assert d == SLAB * LANES and N_EXPERTS <= LANES
    assert depth == 1, "the final rms_norm is fused into the combine kernel of the only layer"
    row2 = lambda a: a.reshape(1, -1)
    l = 0

    tb = _tile(seq, 256)
    tt = _tile(t, 512)
    x2 = x.reshape(t, d)
    u, q, k, v, gates = _in_proj(x2, row2(norm1_g[l]), w_in[l].astype(BF16), row2(gate_b[l]), tm=_tile(t, 512))
    o = _attention(q, k, v, batch=batch, seq=seq, tb=tb, heads=ATTN_HEADS_PER_STEP)
    c = _conv_branch(u, conv_dw_w[l], row2(conv_dw_b[l]), row2(conv_norm_g[l]), row2(conv_norm_b[l]),
                     batch=batch, seq=seq, ts=_tile(seq, 256))
    rw = jnp.pad(router_w[l], ((0, 0), (0, LANES - N_EXPERTS)))
    rw_hi = rw.astype(BF16)
    rw_lo = (rw - rw_hi.astype(F32)).astype(BF16)
    rb = jnp.pad(router_b[l].astype(F32), (0, LANES - N_EXPERTS), constant_values=-1e30).reshape(1, LANES)
    x1, h2s, topi, topw, counts = _mix(x2, c, o, gates, w_conv_out[l].astype(BF16), w_attn_out[l].astype(BF16),
                                       w_out[l].astype(BF16), row2(norm2_g[l]), rw_hi, rw_lo, rb, tm=tt)
    out = _moe(x1, h2s, topi, topw, counts, expert_w1[l], expert_b1[l][:, None, :],
               expert_w2[l], expert_b2[l][:, None, :], row2(final_norm_g), tt=tt,
               tr=_tile(t * TOP_K, 512))
    return out.reshape(batch, seq, d)
```

```python
import functools

import jax
import jax.numpy as jnp
from jax import lax
from jax.experimental import pallas as pl
from jax.experimental.pallas import tpu as pltpu

F32 = jnp.float32
BF16 = jnp.bfloat16

HEAD_DIM = 64
CONV_WIDTH = 31
N_EXPERTS = 32
TOP_K = 4
SWIGLU_LIMIT = 7.0
SWIGLU_ALPHA = 1.702
EPS = 1e-6

LANES = 128
SLAB = 8
ROW_SLAB = 4
HIGH_HALF = 0xFFFF0000
ATTN_HEADS_PER_STEP = 4
ISSUE_GROUP = 8
CONV_HALO = 32
CONV_ROWS = 128
VMEM_LIMIT = 56 * 1024 * 1024

LOG2_E = 1.4426950408889634
SKIP_THRESHOLD = 150.0 * (1.0 + 2.0 ** -7)
SIGN_BIT = 0x80000000


def _params(sem):
    return pltpu.CompilerParams(dimension_semantics=sem, vmem_limit_bytes=VMEM_LIMIT)


def _resident(shape):
    return pl.BlockSpec(shape, lambda *_: (0,) * len(shape), pipeline_mode=pl.Buffered(1))


def _sigmoid(x):
    return 1.0 / (1.0 + jnp.exp(-x))


def _inproj_kernel(x_ref, g_ref, w_ref, gb_ref, u_ref, q_ref, k_ref, v_ref, gate_ref, *, d):
    x = x_ref[...]
    ms = jnp.mean(x * x, axis=-1, keepdims=True)
    h = (x * lax.rsqrt(ms + EPS) * g_ref[...]).astype(BF16)

    def proj(c):
        return jnp.dot(h, w_ref[:, c * d:(c + 1) * d], preferred_element_type=F32)

    u_ref[...] = (proj(0) * _sigmoid(proj(1))).astype(BF16)
    q_ref[...] = (proj(2) * (HEAD_DIM ** -0.5 * LOG2_E)).astype(BF16)
    k_ref[...] = proj(3).astype(BF16)
    v_ref[...] = proj(4).astype(BF16)
    gate_ref[:, :d] = _sigmoid(proj(5) + gb_ref[:, :d]).astype(BF16)
    gate_ref[:, d:] = _sigmoid(proj(6) + gb_ref[:, d:]).astype(BF16)


def _in_proj(x2, norm_g, w_in, gate_b, *, tm):
    t, d = x2.shape
    row = lambda n: pl.BlockSpec((tm, n), lambda i: (i, 0))
    out = lambda n: jax.ShapeDtypeStruct((t, n), BF16)
    return pl.pallas_call(
        functools.partial(_inproj_kernel, d=d),
        grid=(t // tm,),
        in_specs=[row(d), _resident((1, d)), _resident(w_in.shape), _resident((1, 2 * d))],
        out_specs=[row(d), row(d), row(d), row(d), row(2 * d)],
        out_shape=[out(d), out(d), out(d), out(d), out(2 * d)],
        compiler_params=_params(("parallel",)),
        name="in_proj",
    )(x2, norm_g, w_in, gate_b)


def _attn_kernel(q_ref, k_ref, v_ref, o_ref, *, tb, heads):
    i = pl.program_id(2)
    r_iota = lax.broadcasted_iota(jnp.int32, (tb, tb), 0)
    c_iota = lax.broadcasted_iota(jnp.int32, (tb, tb), 1)
    upper = jnp.where(r_iota > c_iota, 1.0, 0.0).astype(BF16)
    causal = c_iota < r_iota

    first_head = lax.broadcasted_iota(jnp.int32, (tb, LANES), 1) < HEAD_DIM
    zeros = jnp.zeros((tb, LANES), BF16)

    def split_heads(x2):
        return jnp.concatenate([jnp.where(first_head, x2, zeros), jnp.where(first_head, zeros, x2)], axis=0)

    def pair_block(p, j, carry, acc, diagonal):
        lanes = slice(p * LANES, (p + 1) * LANES)
        start = pl.multiple_of(j * tb, tb)
        z = lax.dot_general(q_ref[:, lanes], split_heads(k_ref[pl.ds(start, tb), lanes]),
                            (((1,), (1,)), ((), ())), preferred_element_type=F32)
        neg_abs = pltpu.bitcast(pltpu.bitcast(z, jnp.uint32) | jnp.uint32(SIGN_BIT), F32)
        l = jnp.log2(1.0 + jnp.exp2(neg_abs))
        sp = jnp.maximum(z, 0.0) + l
        logsig = z - sp
        if diagonal:
            sp = jnp.where(causal2, sp, 0.0)
        spb = sp.astype(BF16)
        prefix = jnp.dot(jnp.concatenate([spb[:, :tb], spb[:, tb:]], axis=0), upper,
                         preferred_element_type=F32)
        mass = (prefix[0:tb, 0:1] + sp[:, 0:1], prefix[tb:, 0:1] + sp[:, tb:tb + 1])
        later = jnp.concatenate([prefix[0:tb] + carry[0], prefix[tb:] + carry[1]], axis=1)
        a = jnp.exp2(logsig - later)
        if diagonal:
            a = jnp.where(causal2, a, 0.0)
        acc = acc + jnp.dot(a.astype(BF16), split_heads(v_ref[pl.ds(start, tb), lanes]),
                            preferred_element_type=F32)
        return (carry[0] + mass[0], carry[1] + mass[1]), acc

    pairs = heads * HEAD_DIM // LANES
    causal2 = jnp.concatenate([causal, causal], axis=1)

    def blocks(j, carries, accs, diagonal):
        new = [pair_block(p, j, carries[p], accs[p], diagonal) for p in range(pairs)]
        return tuple(c for c, _ in new), tuple(a for _, a in new)

    zero_carry = (jnp.zeros((tb, 1), F32), jnp.zeros((tb, 1), F32))

    def least_mass(carries):
        return jnp.min(functools.reduce(jnp.minimum, [c for pair in carries for c in pair]))

    def leading_blocks(with_previous):
        carries, accs = blocks(i, (zero_carry,) * pairs, (jnp.zeros((tb, LANES), F32),) * pairs, True)
        if with_previous:
            carries, accs = blocks(i - 1, carries, accs, False)
        return least_mass(carries), carries, accs

    least, carries, accs = lax.cond(i > 0, lambda: leading_blocks(True), lambda: leading_blocks(False))

    def cond(state):
        j, least, _, _ = state
        return jnp.logical_and(j >= 0, least < SKIP_THRESHOLD)

    def body(state):
        j, _, carries, accs = state
        carries, accs = blocks(j, carries, accs, False)
        return j - 1, least_mass(carries), carries, accs

    _, _, _, accs = lax.while_loop(cond, body, (i - 2, least, carries, accs))
    for p in range(pairs):
        o_ref[:, p * LANES:(p + 1) * LANES] = accs[p].astype(BF16)


def _attention(q, k, v, *, batch, seq, tb, heads):
    t, d = q.shape
    nq = seq // tb
    width = heads * HEAD_DIM
    qspec = pl.BlockSpec((tb, width), lambda b, hp, i: (b * nq + i, hp))
    kvspec = pl.BlockSpec((seq, width), lambda b, hp, i: (b, hp))
    return pl.pallas_call(
        functools.partial(_attn_kernel, tb=tb, heads=heads),
        grid=(batch, d // width, nq),
        in_specs=[qspec, kvspec, kvspec],
        out_specs=qspec,
        out_shape=jax.ShapeDtypeStruct((t, d), BF16),
        compiler_params=_params(("parallel", "parallel", "arbitrary")),
        name="attention",
    )(q, k, v)


def _conv_kernel(prev_ref, cur_ref, w_ref, b_ref, g_ref, nb_ref, o_ref, win_ref, conv_ref, *, ts, d):
    i = pl.program_id(1)
    prev = prev_ref[...].astype(F32)
    win_ref[0:CONV_HALO, :] = jnp.where(i == 0, 0.0, prev)
    win_ref[CONV_HALO:, :] = cur_ref[...].astype(F32)
    base = CONV_HALO - (CONV_WIDTH - 1)
    rows = CONV_ROWS
    span = rows + CONV_HALO
    for c in range(d // LANES):
        lanes = slice(c * LANES, (c + 1) * LANES)
        for r0 in range(0, ts, rows):
            window = win_ref[r0:r0 + span, lanes]
            acc = jnp.zeros((rows, LANES), F32)
            for shift in range(SLAB):
                shifted = pltpu.roll(window, (span - shift) % span, axis=0) if shift else window
                for w in range(CONV_WIDTH):
                    if (base + w) % SLAB == shift:
                        lead = (base + w) // SLAB * SLAB
                        acc = acc + shifted[lead:lead + rows] * w_ref[w:w + 1, lanes]
            conv_ref[r0:r0 + rows, lanes] = acc + b_ref[:, lanes]
    y = conv_ref[...]
    mu = jnp.mean(y, axis=-1, keepdims=True)
    yc = y - mu
    var = jnp.mean(yc * yc, axis=-1, keepdims=True)
    yn = yc * lax.rsqrt(var + EPS) * g_ref[...] + nb_ref[...]
    o_ref[...] = (yn * _sigmoid(yn)).astype(BF16)


def _conv_branch(u, dw_w, dw_b, norm_g, norm_b, *, batch, seq, ts):
    t, d = u.shape
    ns = seq // ts
    per = ts // CONV_HALO
    cur = pl.BlockSpec((ts, d), lambda b, i: (b * ns + i, 0))
    prev = pl.BlockSpec((CONV_HALO, d), lambda b, i: (jnp.maximum((b * ns + i) * per - 1, 0), 0))
    return pl.pallas_call(
        functools.partial(_conv_kernel, ts=ts, d=d),
        grid=(batch, ns),
        in_specs=[prev, cur, _resident(dw_w.shape), _resident((1, d)), _resident((1, d)), _resident((1, d))],
        out_specs=cur,
        out_shape=jax.ShapeDtypeStruct((t, d), BF16),
        scratch_shapes=[pltpu.VMEM((ts + CONV_HALO, d), F32), pltpu.VMEM((ts, d), F32)],
        compiler_params=_params(("parallel", "parallel")),
        name="conv_branch",
    )(u, u, dw_w, dw_b, norm_g, norm_b)


def _pack_rows(ref, val, rows):
    half = val.shape[1] // 2
    rounded = val.astype(BF16).astype(F32)
    words = (lax.shift_right_logical(pltpu.bitcast(rounded[:, :half], jnp.uint32), jnp.uint32(16))
             | (pltpu.bitcast(rounded[:, half:], jnp.uint32) & jnp.uint32(HIGH_HALF)))
    for c in range(ROW_SLAB):
        ref[pl.ds(c, rows, stride=ROW_SLAB), :] = words[:, c * LANES:(c + 1) * LANES]


def _unpack_rows(ref, rows, lead=()):
    words = [ref[lead + (pl.ds(c, rows, stride=ROW_SLAB), slice(None))] for c in range(ROW_SLAB)]
    low = [pltpu.bitcast(lax.shift_left(w, jnp.uint32(16)), F32) for w in words]
    high = [pltpu.bitcast(w & jnp.uint32(HIGH_HALF), F32) for w in words]
    return jnp.concatenate(low + high, axis=1)


def _mix_kernel(x_ref, c_ref, o_ref, gate_ref, wc_ref, wa_ref, wo_ref, g2_ref, rw_hi_ref, rw_lo_ref, rb_ref,
                x1_ref, h2s_ref, topi_ref, topw_ref, cnt_ref, *, d, tm):
    y_conv = jnp.dot(c_ref[...], wc_ref[...], preferred_element_type=F32)
    y_attn = jnp.dot(o_ref[...], wa_ref[...], preferred_element_type=F32)
    m = gate_ref[:, :d].astype(F32) * y_conv + gate_ref[:, d:].astype(F32) * y_attn
    x1 = x_ref[...] + jnp.dot(m.astype(BF16), wo_ref[...], preferred_element_type=F32)
    x1_ref[...] = x1
    ms = jnp.mean(x1 * x1, axis=-1, keepdims=True)
    h2 = x1 * lax.rsqrt(ms + EPS) * g2_ref[...]
    h2_hi = h2.astype(BF16)
    _pack_rows(h2s_ref, h2, tm)
    h2_lo = (h2 - h2_hi.astype(F32)).astype(BF16)
    logits = (jnp.dot(h2_hi, rw_hi_ref[...], preferred_element_type=F32)
              + jnp.dot(h2_lo, rw_hi_ref[...], preferred_element_type=F32)
              + jnp.dot(h2_hi, rw_lo_ref[...], preferred_element_type=F32)
              + rb_ref[...])
    lane = lax.broadcasted_iota(jnp.int32, logits.shape, 1)
    work = logits
    top = None
    topi = jnp.zeros(logits.shape, jnp.int32)
    topw = jnp.zeros_like(logits)
    chosen = jnp.zeros_like(logits)
    for k in range(TOP_K):
        mx = jnp.max(work, axis=1, keepdims=True)
        idx = jnp.min(jnp.where(work == mx, lane, LANES), axis=1, keepdims=True)
        sel = lane == idx
        if k == 0:
            top = mx
        topi = jnp.where(lane == k, idx, topi)
        topw = jnp.where(lane == k, jnp.exp(mx - top), topw)
        chosen = jnp.where(sel, 1.0, chosen)
        work = jnp.where(sel, -jnp.inf, work)
    topi_ref[...] = topi
    topw_ref[...] = topw / jnp.sum(topw, axis=1, keepdims=True)
    cnt_ref[0] = jnp.sum(chosen, axis=0, keepdims=True)


def _mix(x2, c, o, gates, wc, wa, wo, g2, rw_hi, rw_lo, rb, *, tm):
    t, d = x2.shape
    row = lambda n: pl.BlockSpec((tm, n), lambda i: (i, 0))
    return pl.pallas_call(
        functools.partial(_mix_kernel, d=d, tm=tm),
        grid=(t // tm,),
        in_specs=[row(d), row(d), row(d), row(2 * d), _resident((d, d)), _resident((d, d)), _resident((d, d)),
                  _resident((1, d)), _resident((d, LANES)), _resident((d, LANES)), _resident((1, LANES))],
        out_specs=[row(d), pl.BlockSpec((tm * ROW_SLAB, LANES), lambda i: (i, 0)), row(LANES), row(LANES),
                   pl.BlockSpec((1, 1, LANES), lambda i: (i, 0, 0))],
        out_shape=[jax.ShapeDtypeStruct((t, d), F32), jax.ShapeDtypeStruct((t * ROW_SLAB, LANES), jnp.uint32),
                   jax.ShapeDtypeStruct((t, LANES), jnp.int32), jax.ShapeDtypeStruct((t, LANES), F32),
                   jax.ShapeDtypeStruct((t // tm, 1, LANES), F32)],
        compiler_params=_params(("parallel",)),
        name="mix_router",
    )(x2, c, o, gates, wc, wa, wo, g2, rw_hi, rw_lo, rb)


def _plan_kernel(topi_ref, base_ref, dest_ref, *, tt):
    idx = topi_ref[...]
    lane = lax.broadcasted_iota(jnp.int32, idx.shape, 1)
    onehots = [lane == idx[:, k:k + 1] for k in range(TOP_K)]
    chosen = jnp.zeros(idx.shape, F32)
    for oh in onehots:
        chosen = jnp.where(oh, 1.0, chosen)
    r_iota = lax.broadcasted_iota(jnp.int32, (tt, tt), 0)
    c_iota = lax.broadcasted_iota(jnp.int32, (tt, tt), 1)
    earlier = jnp.where(c_iota < r_iota, 1.0, 0.0).astype(BF16)
    rank = jnp.dot(earlier, chosen.astype(BF16), preferred_element_type=F32)
    pos = rank + base_ref[0]
    dest = jnp.zeros(idx.shape, F32)
    for k, oh in enumerate(onehots):
        dk = jnp.sum(jnp.where(oh, pos, 0.0), axis=1, keepdims=True)
        dest = jnp.where(lane == k, dk, dest)
    dest_ref[...] = dest.T[0:SLAB, :].astype(jnp.int32)


def _plan(topi, base, *, tt):
    t = topi.shape[0]
    nt = t // tt
    return pl.pallas_call(
        functools.partial(_plan_kernel, tt=tt),
        grid=(nt,),
        in_specs=[pl.BlockSpec((tt, LANES), lambda i: (i, 0)), pl.BlockSpec((1, 1, LANES), lambda i: (i, 0, 0))],
        out_specs=pl.BlockSpec((SLAB, tt), lambda i: (i, 0)),
        out_shape=jax.ShapeDtypeStruct((nt * SLAB, tt), jnp.int32),
        compiler_params=_params(("parallel",)),
        name="moe_plan",
    )(topi, base)


def _row_copy(src_ref, src_row, dst_ref, dst_row, sem):
    return pltpu.make_async_copy(src_ref.at[pl.ds(pl.multiple_of(src_row * ROW_SLAB, ROW_SLAB), ROW_SLAB)],
                                 dst_ref.at[pl.ds(pl.multiple_of(dst_row * ROW_SLAB, ROW_SLAB), ROW_SLAB)], sem)


def _dest_load(dest_hbm, dest_smem, isem, tile, slot, tt):
    n = SLAB * tt
    return pltpu.make_async_copy(dest_hbm.at[pl.ds(pl.multiple_of(tile * n, n), n)],
                                 dest_smem.at[pl.ds(pl.multiple_of(slot * n, n), n)], isem)


def _pad_fill_copies(pad_start_ref, pad_cnt_ref, zero_ref, xs_hbm, sem, e, tr):
    cnt = pad_cnt_ref[e]
    out = []
    bit = 1
    while bit < tr:
        first = pad_start_ref[e] + (cnt & (bit - 1))
        copy = pltpu.make_async_copy(zero_ref.at[pl.ds(0, bit * ROW_SLAB)],
                                     xs_hbm.at[pl.ds(pl.multiple_of(first * ROW_SLAB, ROW_SLAB), bit * ROW_SLAB)], sem)
        out.append(((cnt & bit) != 0, copy))
        bit *= 2
    return out


def _dispatch_kernel(pad_start_ref, pad_cnt_ref, dest_hbm, h2s_ref, xs_hbm, dest_smem, zero_ref, sem, isem, zsem,
                     *, tt, tr, n_e):
    i = pl.program_id(0)
    slot = i % 2

    @pl.when(i == 0)
    def _():
        _dest_load(dest_hbm, dest_smem, isem, 0, 0, tt).start()

    @pl.when(i == 0)
    def _():
        zero_ref[...] = jnp.zeros_like(zero_ref)
        for e in range(n_e):
            for pred, copy in _pad_fill_copies(pad_start_ref, pad_cnt_ref, zero_ref, xs_hbm, zsem, e, tr):
                pl.when(pred)(copy.start)

    _dest_load(dest_hbm, dest_smem, isem, i, slot, tt).wait()

    @pl.when(i + 1 < pl.num_programs(0))
    def _():
        _dest_load(dest_hbm, dest_smem, isem, i + 1, 1 - slot, tt).start()

    def issue(g, carry):
        toks = [g * ISSUE_GROUP + j for j in range(ISSUE_GROUP)]
        rows = [[dest_smem[(slot * SLAB + k) * tt + tok] for k in range(TOP_K)] for tok in toks]
        for m, (tok, row) in enumerate(zip(toks, rows)):
            for k in range(TOP_K):
                _row_copy(h2s_ref, tok, xs_hbm, row[k], sem).start(priority=(m * TOP_K + k) % 2)
        return carry
    lax.fori_loop(0, tt // ISSUE_GROUP, issue, 0)

    def drain(tok, carry):
        for k in range(TOP_K):
            _row_copy(h2s_ref, 0, xs_hbm, 0, sem).wait()
        return carry
    lax.fori_loop(0, tt, drain, 0, unroll=8)

    @pl.when(i == pl.num_programs(0) - 1)
    def _():
        for e in range(n_e):
            for pred, copy in _pad_fill_copies(pad_start_ref, pad_cnt_ref, zero_ref, xs_hbm, zsem, e, tr):
                pl.when(pred)(copy.wait)


def _dispatch(pad_start, pad_cnt, dest, h2s, *, rows, tt, tr):
    nt = dest.shape[0] // (SLAB * tt)
    n_e = pad_start.shape[0]
    anyspec = pl.BlockSpec(memory_space=pl.ANY)
    return pl.pallas_call(
        functools.partial(_dispatch_kernel, tt=tt, tr=tr, n_e=n_e),
        grid_spec=pltpu.PrefetchScalarGridSpec(
            num_scalar_prefetch=2, grid=(nt,),
            in_specs=[anyspec, pl.BlockSpec((tt * ROW_SLAB, LANES), lambda i, ps, pc: (i, 0))], out_specs=anyspec,
            scratch_shapes=[pltpu.SMEM((2 * SLAB * tt,), jnp.int32), pltpu.VMEM((tr // 2 * ROW_SLAB, LANES), jnp.uint32),
                            pltpu.SemaphoreType.DMA, pltpu.SemaphoreType.DMA, pltpu.SemaphoreType.DMA]),
        out_shape=jax.ShapeDtypeStruct((rows * ROW_SLAB, LANES), jnp.uint32),
        compiler_params=_params(("arbitrary",)),
        name="moe_dispatch",
    )(pad_start, pad_cnt, dest, h2s)


def _ffn_kernel(te_ref, nu_ref, xs_ref, w1_ref, b1_ref, w2_ref, b2_ref, o_ref, w1b_ref, w2b_ref, *, tr, dff):
    r = pl.program_id(0)

    @pl.when(jnp.logical_or(r == 0, te_ref[r] != te_ref[jnp.maximum(r - 1, 0)]))
    def _():
        w1b_ref[...] = w1_ref[0].astype(BF16)
        w2b_ref[...] = w2_ref[0].astype(BF16)

    @pl.when(r < nu_ref[0])
    def _():
        xs = _unpack_rows(xs_ref, tr).astype(BF16)
        hid = jnp.dot(xs, w1b_ref[...], preferred_element_type=F32) + b1_ref[0]
        glu = jnp.minimum(hid[:, :dff], SWIGLU_LIMIT)
        lin = jnp.clip(hid[:, dff:], -SWIGLU_LIMIT, SWIGLU_LIMIT)
        act = glu * _sigmoid(SWIGLU_ALPHA * glu) * (lin + 1.0)
        out = jnp.dot(act.astype(BF16), w2b_ref[...], preferred_element_type=F32) + b2_ref[0]
        _pack_rows(o_ref, out, tr)


def _ffn(tile_expert, n_used, xs, w1, b1, w2, b2, *, tr):
    n_e, d, dff2 = w1.shape
    n_tiles = tile_expert.shape[0]
    rows = lambda r, te, nu: (jnp.minimum(r, nu[0] - 1), 0)
    per_e = lambda a, b: pl.BlockSpec((1, a, b), lambda r, te, nu: (te[r], 0, 0))
    return pl.pallas_call(
        functools.partial(_ffn_kernel, tr=tr, dff=dff2 // 2),
        grid_spec=pltpu.PrefetchScalarGridSpec(
            num_scalar_prefetch=2, grid=(n_tiles,),
            in_specs=[pl.BlockSpec((tr * ROW_SLAB, LANES), rows), per_e(d, dff2), per_e(1, dff2), per_e(dff2 // 2, d),
                      per_e(1, d)],
            out_specs=pl.BlockSpec((tr * ROW_SLAB, LANES), rows),
            scratch_shapes=[pltpu.VMEM((d, dff2), BF16), pltpu.VMEM((dff2 // 2, d), BF16)]),
        out_shape=jax.ShapeDtypeStruct(xs.shape, jnp.uint32),
        compiler_params=_params(("arbitrary",)),
        name="moe_experts",
    )(tile_expert, n_used, xs, w1, b1, w2, b2)


def _combine_kernel(dest_hbm, ys_hbm, x1_ref, topw_ref, fg_ref, o_ref, dest_smem, buf_ref, sems, isem, *, tt):
    i = pl.program_id(0)
    n = pl.num_programs(0)

    def gather(tile):
        slot = tile % 2
        _dest_load(dest_hbm, dest_smem, isem, tile, slot, tt).wait()

        @pl.when(tile + 1 < n)
        def _():
            _dest_load(dest_hbm, dest_smem, isem, tile + 1, 1 - slot, tt).start()

        def issue(g, carry):
            toks = [g * ISSUE_GROUP + j for j in range(ISSUE_GROUP)]
            rows = [[dest_smem[(slot * SLAB + k) * tt + tok] for k in range(TOP_K)] for tok in toks]
            for m, (tok, row) in enumerate(zip(toks, rows)):
                for k in range(TOP_K):
                    _row_copy(ys_hbm, row[k], buf_ref.at[slot, k], tok, sems.at[slot]).start(
                        priority=(m * TOP_K + k) % 2)
            return carry
        lax.fori_loop(0, tt // ISSUE_GROUP, issue, 0)

    @pl.when(i == 0)
    def _():
        _dest_load(dest_hbm, dest_smem, isem, 0, 0, tt).start()
        gather(0)

    @pl.when(i + 1 < n)
    def _():
        gather(i + 1)

    slot = i % 2

    def drain(tok, carry):
        for k in range(TOP_K):
            _row_copy(ys_hbm, 0, buf_ref.at[slot, k], 0, sems.at[slot]).wait()
        return carry
    lax.fori_loop(0, tt, drain, 0, unroll=8)

    y = x1_ref[...]
    w = topw_ref[...]
    for k in range(TOP_K):
        y = y + w[:, k:k + 1] * _unpack_rows(buf_ref, tt, lead=(slot, k))
    ms = jnp.mean(y * y, axis=-1, keepdims=True)
    o_ref[...] = y * lax.rsqrt(ms + EPS) * fg_ref[...]


def _combine(dest, ys, x1, topw, fg, *, tt):
    t, d = x1.shape
    row = lambda n: pl.BlockSpec((tt, n), lambda i: (i, 0))
    anyspec = pl.BlockSpec(memory_space=pl.ANY)
    return pl.pallas_call(
        functools.partial(_combine_kernel, tt=tt),
        grid=(t // tt,),
        in_specs=[anyspec, anyspec, row(d), row(LANES), _resident((1, d))],
        out_specs=row(d),
        out_shape=jax.ShapeDtypeStruct((t, d), F32),
        scratch_shapes=[pltpu.SMEM((2 * SLAB * tt,), jnp.int32), pltpu.VMEM((2, TOP_K, tt * ROW_SLAB, LANES), jnp.uint32),
                        pltpu.SemaphoreType.DMA((2,)), pltpu.SemaphoreType.DMA],
        compiler_params=_params(("arbitrary",)),
        name="moe_combine",
    )(dest, ys, x1, topw, fg)


def _moe(x1, h2s, topi, topw, counts, w1, b1, w2, b2, fg, *, tt, tr):
    t = x1.shape[0]
    n_e = w1.shape[0]
    cnt = counts[:, 0, :n_e].astype(jnp.int32)
    nt = cnt.shape[0]
    total = jnp.sum(cnt, axis=0)
    padded = (total + tr - 1) // tr * tr
    e_ids = jnp.arange(n_e)
    ends = jnp.sum(jnp.where(e_ids[:, None] <= e_ids[None, :], padded[:, None], 0), axis=0)
    starts = ends - padded
    t_ids = jnp.arange(nt)
    before = jnp.sum(jnp.where((t_ids[:, None] < t_ids[None, :])[:, :, None], cnt[:, None, :], 0), axis=0)
    base = starts[None, :] + before
    base = jnp.pad(base.astype(F32), ((0, 0), (0, LANES - n_e)))[:, None, :]
    n_tiles = (t * TOP_K) // tr + n_e
    n_used = (ends[-1] // tr).astype(jnp.int32)
    r = jnp.minimum(jnp.arange(n_tiles, dtype=jnp.int32), n_used - 1)
    tile_expert = jnp.sum((ends // tr)[None, :] <= r[:, None], axis=1).astype(jnp.int32)

    dest = _plan(topi, base, tt=tt).reshape(-1)
    xs = _dispatch(starts + total, padded - total, dest, h2s, rows=n_tiles * tr, tt=tt, tr=tr)
    ys = _ffn(tile_expert, n_used.reshape(1), xs, w1, b1, w2, b2, tr=tr)
    return _combine(dest, ys, x1, topw, fg, tt=tt)


def _tile(n, want):
    want = min(want, n)
    assert n % want == 0, (n, want)
    return want


def kernel(x, norm1_g, w_in, conv_dw_w, conv_dw_b, conv_norm_g, conv_norm_b, w_conv_out, w_attn_out, gate_b,
           w_out, norm2_g, router_w, router_b, expert_w1, expert_b1, expert_w2, expert_b2, final_norm_g):
    batch, seq, d = x.shape
    depth = w_in.shape[0]
    t = batch * seq
    assert d == 2 * ROW_SLAB * LANES and N_EXPERTS <= LANES
    assert depth == 1, "the final rms_norm is fused into the combine kernel of the only layer"
    row2 = lambda a: a.reshape(1, -1)
    l = 0

    tb = _tile(seq, 256)
    tt = _tile(t, 512)
    x2 = x.reshape(t, d)
    u, q, k, v, gates = _in_proj(x2, row2(norm1_g[l]), w_in[l].astype(BF16), row2(gate_b[l]), tm=_tile(t, 512))
    o = _attention(q, k, v, batch=batch, seq=seq, tb=tb, heads=ATTN_HEADS_PER_STEP)
    c = _conv_branch(u, conv_dw_w[l], row2(conv_dw_b[l]), row2(conv_norm_g[l]), row2(conv_norm_b[l]),
                     batch=batch, seq=seq, ts=_tile(seq, 256))
    rw = jnp.pad(router_w[l], ((0, 0), (0, LANES - N_EXPERTS)))
    rw_hi = rw.astype(BF16)
    rw_lo = (rw - rw_hi.astype(F32)).astype(BF16)
    rb = jnp.pad(router_b[l].astype(F32), (0, LANES - N_EXPERTS), constant_values=-1e30).reshape(1, LANES)
    x1, h2s, topi, topw, counts = _mix(x2, c, o, gates, w_conv_out[l].astype(BF16), w_attn_out[l].astype(BF16),
                                       w_out[l].astype(BF16), row2(norm2_g[l]), rw_hi, rw_lo, rb, tm=tt)
    out = _moe(x1, h2s, topi, topw, counts, expert_w1[l], expert_b1[l][:, None, :],
               expert_w2[l], expert_b2[l][:, None, :], row2(final_norm_g), tt=tt,
               tr=_tile(t * TOP_K, 512))
    return out.reshape(batch, seq, d)
```

```python
import functools

import jax
import jax.numpy as jnp
from jax import lax
from jax.experimental import pallas as pl
from jax.experimental.pallas import tpu as pltpu

F32 = jnp.float32
BF16 = jnp.bfloat16

HEAD_DIM = 64
CONV_WIDTH = 31
N_EXPERTS = 32
TOP_K = 4
SWIGLU_LIMIT = 7.0
SWIGLU_ALPHA = 1.702
EPS = 1e-6

LANES = 128
SLAB = 8
ROW_SLAB = 4
HIGH_HALF = 0xFFFF0000
ATTN_HEADS_PER_STEP = 4
ISSUE_GROUP = 8
CONV_HALO = 32
CONV_ROWS = 128
VMEM_LIMIT = 56 * 1024 * 1024

LOG2_E = 1.4426950408889634
SKIP_THRESHOLD = 150.0 * (1.0 + 2.0 ** -7)
SIGN_BIT = 0x80000000


def _params(sem):
    return pltpu.CompilerParams(dimension_semantics=sem, vmem_limit_bytes=VMEM_LIMIT)


def _resident(shape):
    return pl.BlockSpec(shape, lambda *_: (0,) * len(shape), pipeline_mode=pl.Buffered(1))


def _sigmoid(x):
    return 1.0 / (1.0 + jnp.exp(-x))


def _inproj_kernel(x_ref, g_ref, w_ref, gb_ref, u_ref, q_ref, k_ref, v_ref, gate_ref, *, d):
    x = x_ref[...]
    ms = jnp.mean(x * x, axis=-1, keepdims=True)
    h = (x * lax.rsqrt(ms + EPS) * g_ref[...]).astype(BF16)

    def proj(c):
        return jnp.dot(h, w_ref[:, c * d:(c + 1) * d], preferred_element_type=F32)

    u_ref[...] = (proj(0) * _sigmoid(proj(1))).astype(BF16)
    q_ref[...] = (proj(2) * (HEAD_DIM ** -0.5 * LOG2_E)).astype(BF16)
    k_ref[...] = proj(3).astype(BF16)
    v_ref[...] = proj(4).astype(BF16)
    gate_ref[:, :d] = _sigmoid(proj(5) + gb_ref[:, :d]).astype(BF16)
    gate_ref[:, d:] = _sigmoid(proj(6) + gb_ref[:, d:]).astype(BF16)


def _in_proj(x2, norm_g, w_in, gate_b, *, tm):
    t, d = x2.shape
    row = lambda n: pl.BlockSpec((tm, n), lambda i: (i, 0))
    out = lambda n: jax.ShapeDtypeStruct((t, n), BF16)
    return pl.pallas_call(
        functools.partial(_inproj_kernel, d=d),
        grid=(t // tm,),
        in_specs=[row(d), _resident((1, d)), _resident(w_in.shape), _resident((1, 2 * d))],
        out_specs=[row(d), row(d), row(d), row(d), row(2 * d)],
        out_shape=[out(d), out(d), out(d), out(d), out(2 * d)],
        compiler_params=_params(("parallel",)),
        name="in_proj",
    )(x2, norm_g, w_in, gate_b)


def _attn_kernel(q_ref, k_ref, v_ref, o_ref, acc_ref, mass_ref, *, tb, heads):
    i = pl.program_id(2)
    r_iota = lax.broadcasted_iota(jnp.int32, (tb, tb), 0)
    c_iota = lax.broadcasted_iota(jnp.int32, (tb, tb), 1)
    upper = jnp.where(r_iota > c_iota, 1.0, 0.0).astype(BF16)
    causal = c_iota < r_iota

    first_head = lax.broadcasted_iota(jnp.int32, (tb, LANES), 1) < HEAD_DIM
    zeros = jnp.zeros((tb, LANES), BF16)

    def split_heads(x2):
        return jnp.concatenate([jnp.where(first_head, x2, zeros), jnp.where(first_head, zeros, x2)], axis=0)

    def pair_block(p, j, carry, acc, diagonal):
        lanes = slice(p * LANES, (p + 1) * LANES)
        start = pl.multiple_of(j * tb, tb)
        z = lax.dot_general(q_ref[:, lanes], split_heads(k_ref[pl.ds(start, tb), lanes]),
                            (((1,), (1,)), ((), ())), preferred_element_type=F32)
        neg_abs = pltpu.bitcast(pltpu.bitcast(z, jnp.uint32) | jnp.uint32(SIGN_BIT), F32)
        l = jnp.log2(1.0 + jnp.exp2(neg_abs))
        sp = jnp.maximum(z, 0.0) + l
        logsig = z - sp
        if diagonal:
            sp = jnp.where(causal2, sp, 0.0)
        spb = sp.astype(BF16)
        prefix = jnp.dot(jnp.concatenate([spb[:, :tb], spb[:, tb:]], axis=0), upper,
                         preferred_element_type=F32)
        mass = (prefix[0:tb, 0:1] + sp[:, 0:1], prefix[tb:, 0:1] + sp[:, tb:tb + 1])
        later = jnp.concatenate([prefix[0:tb] + carry[0], prefix[tb:] + carry[1]], axis=1)
        a = jnp.exp2(logsig - later)
        if diagonal:
            a = jnp.where(causal2, a, 0.0)
        acc = acc + jnp.dot(a.astype(BF16), split_heads(v_ref[pl.ds(start, tb), lanes]),
                            preferred_element_type=F32)
        return (carry[0] + mass[0], carry[1] + mass[1]), acc

    pairs = heads * HEAD_DIM // LANES
    causal2 = jnp.concatenate([causal, causal], axis=1)

    def blocks(j, carries, accs, diagonal):
        new = [pair_block(p, j, carries[p], accs[p], diagonal) for p in range(pairs)]
        return tuple(c for c, _ in new), tuple(a for _, a in new)

    zero_carry = (jnp.zeros((tb, 1), F32), jnp.zeros((tb, 1), F32))

    def least_mass(carries):
        return jnp.min(functools.reduce(jnp.minimum, [c for pair in carries for c in pair]))

    def publish(carries, accs):
        for p in range(pairs):
            o_ref[:, p * LANES:(p + 1) * LANES] = accs[p].astype(BF16)
            acc_ref[:, p * LANES:(p + 1) * LANES] = accs[p]
            for h in range(2):
                mass_ref[:, 2 * p + h:2 * p + h + 1] = carries[p][h]
        return least_mass(carries)

    def leading_blocks(with_previous):
        carries, accs = blocks(i, (zero_carry,) * pairs, (jnp.zeros((tb, LANES), F32),) * pairs, True)
        if with_previous:
            carries, accs = blocks(i - 1, carries, accs, False)
        return publish(carries, accs)

    least = lax.cond(i > 0, lambda: leading_blocks(True), lambda: leading_blocks(False))

    def cond(state):
        j, least = state
        return jnp.logical_and(j >= 0, least < SKIP_THRESHOLD)

    def body(state):
        j, _ = state
        carries = tuple((mass_ref[:, 2 * p:2 * p + 1], mass_ref[:, 2 * p + 1:2 * p + 2]) for p in range(pairs))
        accs = tuple(acc_ref[:, p * LANES:(p + 1) * LANES] for p in range(pairs))
        return j - 1, publish(*blocks(j, carries, accs, False))

    lax.while_loop(cond, body, (i - 2, least))


def _attention(q, k, v, *, batch, seq, tb, heads):
    t, d = q.shape
    nq = seq // tb
    width = heads * HEAD_DIM
    qspec = pl.BlockSpec((tb, width), lambda b, hp, i: (b * nq + i, hp))
    kvspec = pl.BlockSpec((seq, width), lambda b, hp, i: (b, hp))
    return pl.pallas_call(
        functools.partial(_attn_kernel, tb=tb, heads=heads),
        grid=(batch, d // width, nq),
        in_specs=[qspec, kvspec, kvspec],
        out_specs=qspec,
        out_shape=jax.ShapeDtypeStruct((t, d), BF16),
        scratch_shapes=[pltpu.VMEM((tb, width), F32), pltpu.VMEM((tb, LANES), F32)],
        compiler_params=_params(("parallel", "parallel", "arbitrary")),
        name="attention",
    )(q, k, v)


def _conv_kernel(prev_ref, cur_ref, w_ref, b_ref, g_ref, nb_ref, o_ref, win_ref, conv_ref, *, ts, d):
    i = pl.program_id(1)
    prev = prev_ref[...].astype(F32)
    win_ref[0:CONV_HALO, :] = jnp.where(i == 0, 0.0, prev)
    win_ref[CONV_HALO:, :] = cur_ref[...].astype(F32)
    base = CONV_HALO - (CONV_WIDTH - 1)
    rows = CONV_ROWS
    span = rows + CONV_HALO
    for c in range(d // LANES):
        lanes = slice(c * LANES, (c + 1) * LANES)
        for r0 in range(0, ts, rows):
            window = win_ref[r0:r0 + span, lanes]
            acc = jnp.zeros((rows, LANES), F32)
            for shift in range(SLAB):
                shifted = pltpu.roll(window, (span - shift) % span, axis=0) if shift else window
                for w in range(CONV_WIDTH):
                    if (base + w) % SLAB == shift:
                        lead = (base + w) // SLAB * SLAB
                        acc = acc + shifted[lead:lead + rows] * w_ref[w:w + 1, lanes]
            conv_ref[r0:r0 + rows, lanes] = acc + b_ref[:, lanes]
    y = conv_ref[...]
    mu = jnp.mean(y, axis=-1, keepdims=True)
    yc = y - mu
    var = jnp.mean(yc * yc, axis=-1, keepdims=True)
    yn = yc * lax.rsqrt(var + EPS) * g_ref[...] + nb_ref[...]
    o_ref[...] = (yn * _sigmoid(yn)).astype(BF16)


def _conv_branch(u, dw_w, dw_b, norm_g, norm_b, *, batch, seq, ts):
    t, d = u.shape
    ns = seq // ts
    per = ts // CONV_HALO
    cur = pl.BlockSpec((ts, d), lambda b, i: (b * ns + i, 0))
    prev = pl.BlockSpec((CONV_HALO, d), lambda b, i: (jnp.maximum((b * ns + i) * per - 1, 0), 0))
    return pl.pallas_call(
        functools.partial(_conv_kernel, ts=ts, d=d),
        grid=(batch, ns),
        in_specs=[prev, cur, _resident(dw_w.shape), _resident((1, d)), _resident((1, d)), _resident((1, d))],
        out_specs=cur,
        out_shape=jax.ShapeDtypeStruct((t, d), BF16),
        scratch_shapes=[pltpu.VMEM((ts + CONV_HALO, d), F32), pltpu.VMEM((ts, d), F32)],
        compiler_params=_params(("parallel", "parallel")),
        name="conv_branch",
    )(u, u, dw_w, dw_b, norm_g, norm_b)


def _pack_rows(ref, val, rows):
    half = val.shape[1] // 2
    rounded = val.astype(BF16).astype(F32)
    words = (lax.shift_right_logical(pltpu.bitcast(rounded[:, :half], jnp.uint32), jnp.uint32(16))
             | (pltpu.bitcast(rounded[:, half:], jnp.uint32) & jnp.uint32(HIGH_HALF)))
    for c in range(ROW_SLAB):
        ref[pl.ds(c, rows, stride=ROW_SLAB), :] = words[:, c * LANES:(c + 1) * LANES]


def _unpack_rows(ref, rows, lead=()):
    words = [ref[lead + (pl.ds(c, rows, stride=ROW_SLAB), slice(None))] for c in range(ROW_SLAB)]
    low = [pltpu.bitcast(lax.shift_left(w, jnp.uint32(16)), F32) for w in words]
    high = [pltpu.bitcast(w & jnp.uint32(HIGH_HALF), F32) for w in words]
    return jnp.concatenate(low + high, axis=1)


def _mix_kernel(x_ref, c_ref, o_ref, gate_ref, wc_ref, wa_ref, wo_ref, g2_ref, rw_hi_ref, rw_lo_ref, rb_ref,
                x1_ref, h2s_ref, topi_ref, topw_ref, cnt_ref, *, d, tm):
    y_conv = jnp.dot(c_ref[...], wc_ref[...], preferred_element_type=F32)
    y_attn = jnp.dot(o_ref[...], wa_ref[...], preferred_element_type=F32)
    m = gate_ref[:, :d].astype(F32) * y_conv + gate_ref[:, d:].astype(F32) * y_attn
    x1 = x_ref[...] + jnp.dot(m.astype(BF16), wo_ref[...], preferred_element_type=F32)
    x1_ref[...] = x1
    ms = jnp.mean(x1 * x1, axis=-1, keepdims=True)
    h2 = x1 * lax.rsqrt(ms + EPS) * g2_ref[...]
    h2_hi = h2.astype(BF16)
    _pack_rows(h2s_ref, h2, tm)
    h2_lo = (h2 - h2_hi.astype(F32)).astype(BF16)
    logits = (jnp.dot(h2_hi, rw_hi_ref[...], preferred_element_type=F32)
              + jnp.dot(h2_lo, rw_hi_ref[...], preferred_element_type=F32)
              + jnp.dot(h2_hi, rw_lo_ref[...], preferred_element_type=F32)
              + rb_ref[...])
    lane = lax.broadcasted_iota(jnp.int32, logits.shape, 1)
    work = logits
    top = None
    topi = jnp.zeros(logits.shape, jnp.int32)
    topw = jnp.zeros_like(logits)
    chosen = jnp.zeros_like(logits)
    for k in range(TOP_K):
        mx = jnp.max(work, axis=1, keepdims=True)
        idx = jnp.min(jnp.where(work == mx, lane, LANES), axis=1, keepdims=True)
        sel = lane == idx
        if k == 0:
            top = mx
        topi = jnp.where(lane == k, idx, topi)
        topw = jnp.where(lane == k, jnp.exp(mx - top), topw)
        chosen = jnp.where(sel, 1.0, chosen)
        work = jnp.where(sel, -jnp.inf, work)
    topi_ref[...] = topi
    topw_ref[...] = topw / jnp.sum(topw, axis=1, keepdims=True)
    cnt_ref[0] = jnp.sum(chosen, axis=0, keepdims=True)


def _mix(x2, c, o, gates, wc, wa, wo, g2, rw_hi, rw_lo, rb, *, tm):
    t, d = x2.shape
    row = lambda n: pl.BlockSpec((tm, n), lambda i: (i, 0))
    return pl.pallas_call(
        functools.partial(_mix_kernel, d=d, tm=tm),
        grid=(t // tm,),
        in_specs=[row(d), row(d), row(d), row(2 * d), _resident((d, d)), _resident((d, d)), _resident((d, d)),
                  _resident((1, d)), _resident((d, LANES)), _resident((d, LANES)), _resident((1, LANES))],
        out_specs=[row(d), pl.BlockSpec((tm * ROW_SLAB, LANES), lambda i: (i, 0)), row(LANES), row(LANES),
                   pl.BlockSpec((1, 1, LANES), lambda i: (i, 0, 0))],
        out_shape=[jax.ShapeDtypeStruct((t, d), F32), jax.ShapeDtypeStruct((t * ROW_SLAB, LANES), jnp.uint32),
                   jax.ShapeDtypeStruct((t, LANES), jnp.int32), jax.ShapeDtypeStruct((t, LANES), F32),
                   jax.ShapeDtypeStruct((t // tm, 1, LANES), F32)],
        compiler_params=_params(("parallel",)),
        name="mix_router",
    )(x2, c, o, gates, wc, wa, wo, g2, rw_hi, rw_lo, rb)


def _plan_kernel(topi_ref, base_ref, dest_ref, *, tt):
    idx = topi_ref[...]
    lane = lax.broadcasted_iota(jnp.int32, idx.shape, 1)
    onehots = [lane == idx[:, k:k + 1] for k in range(TOP_K)]
    chosen = jnp.zeros(idx.shape, F32)
    for oh in onehots:
        chosen = jnp.where(oh, 1.0, chosen)
    r_iota = lax.broadcasted_iota(jnp.int32, (tt, tt), 0)
    c_iota = lax.broadcasted_iota(jnp.int32, (tt, tt), 1)
    earlier = jnp.where(c_iota < r_iota, 1.0, 0.0).astype(BF16)
    rank = jnp.dot(earlier, chosen.astype(BF16), preferred_element_type=F32)
    pos = rank + base_ref[0]
    dest = jnp.zeros(idx.shape, F32)
    for k, oh in enumerate(onehots):
        dk = jnp.sum(jnp.where(oh, pos, 0.0), axis=1, keepdims=True)
        dest = jnp.where(lane == k, dk, dest)
    dest_ref[...] = dest.T[0:SLAB, :].astype(jnp.int32)


def _plan(topi, base, *, tt):
    t = topi.shape[0]
    nt = t // tt
    return pl.pallas_call(
        functools.partial(_plan_kernel, tt=tt),
        grid=(nt,),
        in_specs=[pl.BlockSpec((tt, LANES), lambda i: (i, 0)), pl.BlockSpec((1, 1, LANES), lambda i: (i, 0, 0))],
        out_specs=pl.BlockSpec((SLAB, tt), lambda i: (i, 0)),
        out_shape=jax.ShapeDtypeStruct((nt * SLAB, tt), jnp.int32),
        compiler_params=_params(("parallel",)),
        name="moe_plan",
    )(topi, base)


def _row_copy(src_ref, src_row, dst_ref, dst_row, sem):
    return pltpu.make_async_copy(src_ref.at[pl.ds(pl.multiple_of(src_row * ROW_SLAB, ROW_SLAB), ROW_SLAB)],
                                 dst_ref.at[pl.ds(pl.multiple_of(dst_row * ROW_SLAB, ROW_SLAB), ROW_SLAB)], sem)


def _dest_load(dest_hbm, dest_smem, isem, tile, slot, tt):
    n = SLAB * tt
    return pltpu.make_async_copy(dest_hbm.at[pl.ds(pl.multiple_of(tile * n, n), n)],
                                 dest_smem.at[pl.ds(pl.multiple_of(slot * n, n), n)], isem)


def _pad_fill_copies(pad_start_ref, pad_cnt_ref, zero_ref, xs_hbm, sem, e, tr):
    cnt = pad_cnt_ref[e]
    out = []
    bit = 1
    while bit < tr:
        first = pad_start_ref[e] + (cnt & (bit - 1))
        copy = pltpu.make_async_copy(zero_ref.at[pl.ds(0, bit * ROW_SLAB)],
                                     xs_hbm.at[pl.ds(pl.multiple_of(first * ROW_SLAB, ROW_SLAB), bit * ROW_SLAB)], sem)
        out.append(((cnt & bit) != 0, copy))
        bit *= 2
    return out


def _dispatch_kernel(pad_start_ref, pad_cnt_ref, dest_hbm, h2s_ref, xs_hbm, dest_smem, zero_ref, sem, isem, zsem,
                     *, tt, tr, n_e):
    i = pl.program_id(0)
    slot = i % 2

    @pl.when(i == 0)
    def _():
        _dest_load(dest_hbm, dest_smem, isem, 0, 0, tt).start()

    @pl.when(i == 0)
    def _():
        zero_ref[...] = jnp.zeros_like(zero_ref)
        for e in range(n_e):
            for pred, copy in _pad_fill_copies(pad_start_ref, pad_cnt_ref, zero_ref, xs_hbm, zsem, e, tr):
                pl.when(pred)(copy.start)

    _dest_load(dest_hbm, dest_smem, isem, i, slot, tt).wait()

    @pl.when(i + 1 < pl.num_programs(0))
    def _():
        _dest_load(dest_hbm, dest_smem, isem, i + 1, 1 - slot, tt).start()

    def issue(g, carry):
        toks = [g * ISSUE_GROUP + j for j in range(ISSUE_GROUP)]
        rows = [[dest_smem[(slot * SLAB + k) * tt + tok] for k in range(TOP_K)] for tok in toks]
        for m, (tok, row) in enumerate(zip(toks, rows)):
            for k in range(TOP_K):
                _row_copy(h2s_ref, tok, xs_hbm, row[k], sem).start(priority=(m * TOP_K + k) % 2)
        return carry
    lax.fori_loop(0, tt // ISSUE_GROUP, issue, 0)

    def drain(tok, carry):
        for k in range(TOP_K):
            _row_copy(h2s_ref, 0, xs_hbm, 0, sem).wait()
        return carry
    lax.fori_loop(0, tt, drain, 0, unroll=8)

    @pl.when(i == pl.num_programs(0) - 1)
    def _():
        for e in range(n_e):
            for pred, copy in _pad_fill_copies(pad_start_ref, pad_cnt_ref, zero_ref, xs_hbm, zsem, e, tr):
                pl.when(pred)(copy.wait)


def _dispatch(pad_start, pad_cnt, dest, h2s, *, rows, tt, tr):
    nt = dest.shape[0] // (SLAB * tt)
    n_e = pad_start.shape[0]
    anyspec = pl.BlockSpec(memory_space=pl.ANY)
    return pl.pallas_call(
        functools.partial(_dispatch_kernel, tt=tt, tr=tr, n_e=n_e),
        grid_spec=pltpu.PrefetchScalarGridSpec(
            num_scalar_prefetch=2, grid=(nt,),
            in_specs=[anyspec, pl.BlockSpec((tt * ROW_SLAB, LANES), lambda i, ps, pc: (i, 0))], out_specs=anyspec,
            scratch_shapes=[pltpu.SMEM((2 * SLAB * tt,), jnp.int32), pltpu.VMEM((tr // 2 * ROW_SLAB, LANES), jnp.uint32),
                            pltpu.SemaphoreType.DMA, pltpu.SemaphoreType.DMA, pltpu.SemaphoreType.DMA]),
        out_shape=jax.ShapeDtypeStruct((rows * ROW_SLAB, LANES), jnp.uint32),
        compiler_params=_params(("arbitrary",)),
        name="moe_dispatch",
    )(pad_start, pad_cnt, dest, h2s)


def _ffn_kernel(te_ref, nu_ref, xs_ref, w1_ref, b1_ref, w2_ref, b2_ref, o_ref, w1b_ref, w2b_ref, *, tr, dff):
    r = pl.program_id(0)

    @pl.when(jnp.logical_or(r == 0, te_ref[r] != te_ref[jnp.maximum(r - 1, 0)]))
    def _():
        w1b_ref[...] = w1_ref[0].astype(BF16)
        w2b_ref[...] = w2_ref[0].astype(BF16)

    @pl.when(r < nu_ref[0])
    def _():
        xs = _unpack_rows(xs_ref, tr).astype(BF16)
        hid = jnp.dot(xs, w1b_ref[...], preferred_element_type=F32) + b1_ref[0]
        glu = jnp.minimum(hid[:, :dff], SWIGLU_LIMIT)
        lin = jnp.clip(hid[:, dff:], -SWIGLU_LIMIT, SWIGLU_LIMIT)
        act = glu * _sigmoid(SWIGLU_ALPHA * glu) * (lin + 1.0)
        out = jnp.dot(act.astype(BF16), w2b_ref[...], preferred_element_type=F32) + b2_ref[0]
        _pack_rows(o_ref, out, tr)


def _ffn(tile_expert, n_used, xs, w1, b1, w2, b2, *, tr):
    n_e, d, dff2 = w1.shape
    n_tiles = tile_expert.shape[0]
    rows = lambda r, te, nu: (jnp.minimum(r, nu[0] - 1), 0)
    per_e = lambda a, b: pl.BlockSpec((1, a, b), lambda r, te, nu: (te[r], 0, 0))
    return pl.pallas_call(
        functools.partial(_ffn_kernel, tr=tr, dff=dff2 // 2),
        grid_spec=pltpu.PrefetchScalarGridSpec(
            num_scalar_prefetch=2, grid=(n_tiles,),
            in_specs=[pl.BlockSpec((tr * ROW_SLAB, LANES), rows), per_e(d, dff2), per_e(1, dff2), per_e(dff2 // 2, d),
                      per_e(1, d)],
            out_specs=pl.BlockSpec((tr * ROW_SLAB, LANES), rows),
            scratch_shapes=[pltpu.VMEM((d, dff2), BF16), pltpu.VMEM((dff2 // 2, d), BF16)]),
        out_shape=jax.ShapeDtypeStruct(xs.shape, jnp.uint32),
        compiler_params=_params(("arbitrary",)),
        name="moe_experts",
    )(tile_expert, n_used, xs, w1, b1, w2, b2)


def _combine_kernel(dest_hbm, ys_hbm, x1_ref, topw_ref, fg_ref, o_ref, dest_smem, buf_ref, sems, isem, *, tt):
    i = pl.program_id(0)
    n = pl.num_programs(0)

    def gather(tile):
        slot = tile % 2
        _dest_load(dest_hbm, dest_smem, isem, tile, slot, tt).wait()

        @pl.when(tile + 1 < n)
        def _():
            _dest_load(dest_hbm, dest_smem, isem, tile + 1, 1 - slot, tt).start()

        def issue(g, carry):
            toks = [g * ISSUE_GROUP + j for j in range(ISSUE_GROUP)]
            rows = [[dest_smem[(slot * SLAB + k) * tt + tok] for k in range(TOP_K)] for tok in toks]
            for m, (tok, row) in enumerate(zip(toks, rows)):
                for k in range(TOP_K):
                    _row_copy(ys_hbm, row[k], buf_ref.at[slot, k], tok, sems.at[slot]).start(
                        priority=(m * TOP_K + k) % 2)
            return carry
        lax.fori_loop(0, tt // ISSUE_GROUP, issue, 0)

    @pl.when(i == 0)
    def _():
        _dest_load(dest_hbm, dest_smem, isem, 0, 0, tt).start()
        gather(0)

    @pl.when(i + 1 < n)
    def _():
        gather(i + 1)

    slot = i % 2

    def drain(tok, carry):
        for k in range(TOP_K):
            _row_copy(ys_hbm, 0, buf_ref.at[slot, k], 0, sems.at[slot]).wait()
        return carry
    lax.fori_loop(0, tt, drain, 0, unroll=8)

    y = x1_ref[...]
    w = topw_ref[...]
    for k in range(TOP_K):
        y = y + w[:, k:k + 1] * _unpack_rows(buf_ref, tt, lead=(slot, k))
    ms = jnp.mean(y * y, axis=-1, keepdims=True)
    o_ref[...] = y * lax.rsqrt(ms + EPS) * fg_ref[...]


def _combine(dest, ys, x1, topw, fg, *, tt):
    t, d = x1.shape
    row = lambda n: pl.BlockSpec((tt, n), lambda i: (i, 0))
    anyspec = pl.BlockSpec(memory_space=pl.ANY)
    return pl.pallas_call(
        functools.partial(_combine_kernel, tt=tt),
        grid=(t // tt,),
        in_specs=[anyspec, anyspec, row(d), row(LANES), _resident((1, d))],
        out_specs=row(d),
        out_shape=jax.ShapeDtypeStruct((t, d), F32),
        scratch_shapes=[pltpu.SMEM((2 * SLAB * tt,), jnp.int32), pltpu.VMEM((2, TOP_K, tt * ROW_SLAB, LANES), jnp.uint32),
                        pltpu.SemaphoreType.DMA((2,)), pltpu.SemaphoreType.DMA],
        compiler_params=_params(("arbitrary",)),
        name="moe_combine",
    )(dest, ys, x1, topw, fg)


def _moe(x1, h2s, topi, topw, counts, w1, b1, w2, b2, fg, *, tt, tr):
    t = x1.shape[0]
    n_e = w1.shape[0]
    cnt = counts[:, 0, :n_e].astype(jnp.int32)
    nt = cnt.shape[0]
    total = jnp.sum(cnt, axis=0)
    padded = (total + tr - 1) // tr * tr
    e_ids = jnp.arange(n_e)
    ends = jnp.sum(jnp.where(e_ids[:, None] <= e_ids[None, :], padded[:, None], 0), axis=0)
    starts = ends - padded
    t_ids = jnp.arange(nt)
    before = jnp.sum(jnp.where((t_ids[:, None] < t_ids[None, :])[:, :, None], cnt[:, None, :], 0), axis=0)
    base = starts[None, :] + before
    base = jnp.pad(base.astype(F32), ((0, 0), (0, LANES - n_e)))[:, None, :]
    n_tiles = (t * TOP_K) // tr + n_e
    n_used = (ends[-1] // tr).astype(jnp.int32)
    r = jnp.minimum(jnp.arange(n_tiles, dtype=jnp.int32), n_used - 1)
    tile_expert = jnp.sum((ends // tr)[None, :] <= r[:, None], axis=1).astype(jnp.int32)

    dest = _plan(topi, base, tt=tt).reshape(-1)
    xs = _dispatch(starts + total, padded - total, dest, h2s, rows=n_tiles * tr, tt=tt, tr=tr)
    ys = _ffn(tile_expert, n_used.reshape(1), xs, w1, b1, w2, b2, tr=tr)
    return _combine(dest, ys, x1, topw, fg, tt=tt)


def _tile(n, want):
    want = min(want, n)
    assert n % want == 0, (n, want)
    return want


def kernel(x, norm1_g, w_in, conv_dw_w, conv_dw_b, conv_norm_g, conv_norm_b, w_conv_out, w_attn_out, gate_b,
           w_out, norm2_g, router_w, router_b, expert_w1, expert_b1, expert_w2, expert_b2, final_norm_g):
    batch, seq, d = x.shape
    depth = w_in.shape[0]
    t = batch * seq
    assert d == 2 * ROW_SLAB * LANES and N_EXPERTS <= LANES
    assert depth == 1, "the final rms_norm is fused into the combine kernel of the only layer"
    row2 = lambda a: a.reshape(1, -1)
    l = 0

    tb = _tile(seq, 256)
    tt = _tile(t, 512)
    x2 = x.reshape(t, d)
    u, q, k, v, gates = _in_proj(x2, row2(norm1_g[l]), w_in[l].astype(BF16), row2(gate_b[l]), tm=_tile(t, 512))
    o = _attention(q, k, v, batch=batch, seq=seq, tb=tb, heads=ATTN_HEADS_PER_STEP)
    c = _conv_branch(u, conv_dw_w[l], row2(conv_dw_b[l]), row2(conv_norm_g[l]), row2(conv_norm_b[l]),
                     batch=batch, seq=seq, ts=_tile(seq, 256))
    rw = jnp.pad(router_w[l], ((0, 0), (0, LANES - N_EXPERTS)))
    rw_hi = rw.astype(BF16)
    rw_lo = (rw - rw_hi.astype(F32)).astype(BF16)
    rb = jnp.pad(router_b[l].astype(F32), (0, LANES - N_EXPERTS), constant_values=-1e30).reshape(1, LANES)
    x1, h2s, topi, topw, counts = _mix(x2, c, o, gates, w_conv_out[l].astype(BF16), w_attn_out[l].astype(BF16),
                                       w_out[l].astype(BF16), row2(norm2_g[l]), rw_hi, rw_lo, rb, tm=tt)
    out = _moe(x1, h2s, topi, topw, counts, expert_w1[l], expert_b1[l][:, None, :],
               expert_w2[l], expert_b2[l][:, None, :], row2(final_norm_g), tt=tt,
               tr=_tile(t * TOP_K, 512))
    return out.reshape(batch, seq, d)
```

```python
import functools
from typing import NamedTuple

import jax
import jax.numpy as jnp
from jax import lax
from jax.experimental import pallas as pl
from jax.experimental.pallas import tpu as pltpu

F32 = jnp.float32
BF16 = jnp.bfloat16

HEAD_DIM = 64
CONV_WIDTH = 31
N_EXPERTS = 32
TOP_K = 4
SWIGLU_LIMIT = 7.0
SWIGLU_ALPHA = 1.702
EPS = 1e-6

LANES = 128
SLAB = 8
ROW_SLAB = 4
HIGH_HALF = 0xFFFF0000
ATTN_HEADS_PER_STEP = 4
ISSUE_GROUP = 8
CONV_HALO = 32
CONV_ROWS = 128
VMEM_LIMIT = 56 * 1024 * 1024

LOG2_E = 1.4426950408889634
SKIP_THRESHOLD = 150.0 * (1.0 + 2.0 ** -7)
SIGN_BIT = 0x80000000


def _params(sem):
    return pltpu.CompilerParams(dimension_semantics=sem, vmem_limit_bytes=VMEM_LIMIT)


def _resident(shape):
    return pl.BlockSpec(shape, lambda *_: (0,) * len(shape), pipeline_mode=pl.Buffered(1))


def _sigmoid(x):
    return 1.0 / (1.0 + jnp.exp(-x))


def _inproj_kernel(x_ref, g_ref, w_ref, gb_ref, u_ref, q_ref, k_ref, v_ref, gate_ref, *, d):
    x = x_ref[...]
    ms = jnp.mean(x * x, axis=-1, keepdims=True)
    h = (x * lax.rsqrt(ms + EPS) * g_ref[...]).astype(BF16)

    def proj(c):
        return jnp.dot(h, w_ref[:, c * d:(c + 1) * d], preferred_element_type=F32)

    u_ref[...] = (proj(0) * _sigmoid(proj(1))).astype(BF16)
    q_ref[...] = (proj(2) * (HEAD_DIM ** -0.5 * LOG2_E)).astype(BF16)
    k_ref[...] = proj(3).astype(BF16)
    v_ref[...] = proj(4).astype(BF16)
    gate_ref[:, :d] = _sigmoid(proj(5) + gb_ref[:, :d]).astype(BF16)
    gate_ref[:, d:] = _sigmoid(proj(6) + gb_ref[:, d:]).astype(BF16)


def _in_proj(x2, norm_g, w_in, gate_b, *, tm):
    t, d = x2.shape
    row = lambda n: pl.BlockSpec((tm, n), lambda i: (i, 0))
    out = lambda n: jax.ShapeDtypeStruct((t, n), BF16)
    return pl.pallas_call(
        functools.partial(_inproj_kernel, d=d),
        grid=(t // tm,),
        in_specs=[row(d), _resident((1, d)), _resident(w_in.shape), _resident((1, 2 * d))],
        out_specs=[row(d), row(d), row(d), row(d), row(2 * d)],
        out_shape=[out(d), out(d), out(d), out(d), out(2 * d)],
        compiler_params=_params(("parallel",)),
        name="in_proj",
    )(x2, norm_g, w_in, gate_b)


def _attn_kernel(q_ref, k_ref, v_ref, o_ref, acc_ref, mass_ref, *, tb, heads):
    i = pl.program_id(2)
    r_iota = lax.broadcasted_iota(jnp.int32, (tb, tb), 0)
    c_iota = lax.broadcasted_iota(jnp.int32, (tb, tb), 1)
    upper = jnp.where(r_iota > c_iota, 1.0, 0.0).astype(BF16)
    causal = c_iota < r_iota

    first_head = lax.broadcasted_iota(jnp.int32, (tb, LANES), 1) < HEAD_DIM
    zeros = jnp.zeros((tb, LANES), BF16)

    def split_heads(x2):
        return jnp.concatenate([jnp.where(first_head, x2, zeros), jnp.where(first_head, zeros, x2)], axis=0)

    def pair_block(p, j, carry, acc, diagonal):
        lanes = slice(p * LANES, (p + 1) * LANES)
        start = pl.multiple_of(j * tb, tb)
        z = lax.dot_general(q_ref[:, lanes], split_heads(k_ref[pl.ds(start, tb), lanes]),
                            (((1,), (1,)), ((), ())), preferred_element_type=F32)
        neg_abs = pltpu.bitcast(pltpu.bitcast(z, jnp.uint32) | jnp.uint32(SIGN_BIT), F32)
        l = jnp.log2(1.0 + jnp.exp2(neg_abs))
        sp = jnp.maximum(z, 0.0) + l
        logsig = z - sp
        if diagonal:
            sp = jnp.where(causal2, sp, 0.0)
        spb = sp.astype(BF16)
        prefix = jnp.dot(jnp.concatenate([spb[:, :tb], spb[:, tb:]], axis=0), upper,
                         preferred_element_type=F32)
        mass = (prefix[0:tb, 0:1] + sp[:, 0:1], prefix[tb:, 0:1] + sp[:, tb:tb + 1])
        later = jnp.concatenate([prefix[0:tb] + carry[0], prefix[tb:] + carry[1]], axis=1)
        a = jnp.exp2(logsig - later)
        if diagonal:
            a = jnp.where(causal2, a, 0.0)
        acc = acc + jnp.dot(a.astype(BF16), split_heads(v_ref[pl.ds(start, tb), lanes]),
                            preferred_element_type=F32)
        return (carry[0] + mass[0], carry[1] + mass[1]), acc

    pairs = heads * HEAD_DIM // LANES
    causal2 = jnp.concatenate([causal, causal], axis=1)

    def blocks(j, carries, accs, diagonal):
        new = [pair_block(p, j, carries[p], accs[p], diagonal) for p in range(pairs)]
        return tuple(c for c, _ in new), tuple(a for _, a in new)

    zero_carry = (jnp.zeros((tb, 1), F32), jnp.zeros((tb, 1), F32))

    def least_mass(carries):
        return jnp.min(functools.reduce(jnp.minimum, [c for pair in carries for c in pair]))

    def publish(carries, accs):
        for p in range(pairs):
            o_ref[:, p * LANES:(p + 1) * LANES] = accs[p].astype(BF16)
            acc_ref[:, p * LANES:(p + 1) * LANES] = accs[p]
            for h in range(2):
                mass_ref[:, 2 * p + h:2 * p + h + 1] = carries[p][h]
        return least_mass(carries)

    def leading_blocks(with_previous):
        carries, accs = blocks(i, (zero_carry,) * pairs, (jnp.zeros((tb, LANES), F32),) * pairs, True)
        if with_previous:
            carries, accs = blocks(i - 1, carries, accs, False)
        return publish(carries, accs)

    least = lax.cond(i > 0, lambda: leading_blocks(True), lambda: leading_blocks(False))

    def cond(state):
        j, least = state
        return jnp.logical_and(j >= 0, least < SKIP_THRESHOLD)

    def body(state):
        j, _ = state
        carries = tuple((mass_ref[:, 2 * p:2 * p + 1], mass_ref[:, 2 * p + 1:2 * p + 2]) for p in range(pairs))
        accs = tuple(acc_ref[:, p * LANES:(p + 1) * LANES] for p in range(pairs))
        return j - 1, publish(*blocks(j, carries, accs, False))

    lax.while_loop(cond, body, (i - 2, least))


def _attention(q, k, v, *, batch, seq, tb, heads):
    t, d = q.shape
    nq = seq // tb
    width = heads * HEAD_DIM
    qspec = pl.BlockSpec((tb, width), lambda b, hp, i: (b * nq + i, hp))
    kvspec = pl.BlockSpec((seq, width), lambda b, hp, i: (b, hp))
    return pl.pallas_call(
        functools.partial(_attn_kernel, tb=tb, heads=heads),
        grid=(batch, d // width, nq),
        in_specs=[qspec, kvspec, kvspec],
        out_specs=qspec,
        out_shape=jax.ShapeDtypeStruct((t, d), BF16),
        scratch_shapes=[pltpu.VMEM((tb, width), F32), pltpu.VMEM((tb, LANES), F32)],
        compiler_params=_params(("parallel", "parallel", "arbitrary")),
        name="attention",
    )(q, k, v)


def _conv_kernel(prev_ref, cur_ref, w_ref, b_ref, g_ref, nb_ref, o_ref, win_ref, conv_ref, *, ts, d):
    i = pl.program_id(1)
    prev = prev_ref[...].astype(F32)
    win_ref[0:CONV_HALO, :] = jnp.where(i == 0, 0.0, prev)
    win_ref[CONV_HALO:, :] = cur_ref[...].astype(F32)
    base = CONV_HALO - (CONV_WIDTH - 1)
    rows = CONV_ROWS
    span = rows + CONV_HALO
    for c in range(d // LANES):
        lanes = slice(c * LANES, (c + 1) * LANES)
        for r0 in range(0, ts, rows):
            window = win_ref[r0:r0 + span, lanes]
            acc = jnp.zeros((rows, LANES), F32)
            for shift in range(SLAB):
                shifted = pltpu.roll(window, (span - shift) % span, axis=0) if shift else window
                for w in range(CONV_WIDTH):
                    if (base + w) % SLAB == shift:
                        lead = (base + w) // SLAB * SLAB
                        acc = acc + shifted[lead:lead + rows] * w_ref[w:w + 1, lanes]
            conv_ref[r0:r0 + rows, lanes] = acc + b_ref[:, lanes]
    y = conv_ref[...]
    mu = jnp.mean(y, axis=-1, keepdims=True)
    yc = y - mu
    var = jnp.mean(yc * yc, axis=-1, keepdims=True)
    yn = yc * lax.rsqrt(var + EPS) * g_ref[...] + nb_ref[...]
    o_ref[...] = (yn * _sigmoid(yn)).astype(BF16)


def _conv_branch(u, dw_w, dw_b, norm_g, norm_b, *, batch, seq, ts):
    t, d = u.shape
    ns = seq // ts
    per = ts // CONV_HALO
    cur = pl.BlockSpec((ts, d), lambda b, i: (b * ns + i, 0))
    prev = pl.BlockSpec((CONV_HALO, d), lambda b, i: (jnp.maximum((b * ns + i) * per - 1, 0), 0))
    return pl.pallas_call(
        functools.partial(_conv_kernel, ts=ts, d=d),
        grid=(batch, ns),
        in_specs=[prev, cur, _resident(dw_w.shape), _resident((1, d)), _resident((1, d)), _resident((1, d))],
        out_specs=cur,
        out_shape=jax.ShapeDtypeStruct((t, d), BF16),
        scratch_shapes=[pltpu.VMEM((ts + CONV_HALO, d), F32), pltpu.VMEM((ts, d), F32)],
        compiler_params=_params(("parallel", "parallel")),
        name="conv_branch",
    )(u, u, dw_w, dw_b, norm_g, norm_b)


def _pack_rows(ref, val, rows):
    half = val.shape[1] // 2
    rounded = val.astype(BF16).astype(F32)
    words = (lax.shift_right_logical(pltpu.bitcast(rounded[:, :half], jnp.uint32), jnp.uint32(16))
             | (pltpu.bitcast(rounded[:, half:], jnp.uint32) & jnp.uint32(HIGH_HALF)))
    for c in range(ROW_SLAB):
        ref[pl.ds(c, rows, stride=ROW_SLAB), :] = words[:, c * LANES:(c + 1) * LANES]


def _unpack_rows(ref, rows, lead=()):
    words = [ref[lead + (pl.ds(c, rows, stride=ROW_SLAB), slice(None))] for c in range(ROW_SLAB)]
    low = [pltpu.bitcast(lax.shift_left(w, jnp.uint32(16)), F32) for w in words]
    high = [pltpu.bitcast(w & jnp.uint32(HIGH_HALF), F32) for w in words]
    return jnp.concatenate(low + high, axis=1)


def _mix_kernel(x_ref, c_ref, o_ref, gate_ref, wc_ref, wa_ref, wo_ref, g2_ref, rw_ref, rb_ref,
                x1_ref, h2s_ref, topi_ref, topw_ref, cnt_ref, *, d, tm):
    y_conv = jnp.dot(c_ref[...], wc_ref[...], preferred_element_type=F32)
    y_attn = jnp.dot(o_ref[...], wa_ref[...], preferred_element_type=F32)
    m = gate_ref[:, :d].astype(F32) * y_conv + gate_ref[:, d:].astype(F32) * y_attn
    x1 = x_ref[...] + jnp.dot(m.astype(BF16), wo_ref[...], preferred_element_type=F32)
    x1_ref[...] = x1
    ms = jnp.mean(x1 * x1, axis=-1, keepdims=True)
    h2 = x1 * lax.rsqrt(ms + EPS) * g2_ref[...]
    h2_hi = h2.astype(BF16)
    _pack_rows(h2s_ref, h2, tm)
    h2_lo = (h2 - h2_hi.astype(F32)).astype(BF16)
    both = jnp.dot(h2_hi, rw_ref[...], preferred_element_type=F32)
    logits = (both[:, :LANES] + both[:, LANES:]
              + jnp.dot(h2_lo, rw_ref[:, :LANES], preferred_element_type=F32)
              + rb_ref[...])
    lane = lax.broadcasted_iota(jnp.int32, logits.shape, 1)
    work = logits
    top = None
    topi = jnp.zeros(logits.shape, jnp.int32)
    topw = jnp.zeros_like(logits)
    chosen = jnp.zeros_like(logits)
    for k in range(TOP_K):
        mx = jnp.max(work, axis=1, keepdims=True)
        idx = jnp.min(jnp.where(work == mx, lane, LANES), axis=1, keepdims=True)
        sel = lane == idx
        if k == 0:
            top = mx
        topi = jnp.where(lane == k, idx, topi)
        topw = jnp.where(lane == k, jnp.exp(mx - top), topw)
        chosen = jnp.where(sel, 1.0, chosen)
        work = jnp.where(sel, -jnp.inf, work)
    topi_ref[...] = topi
    topw_ref[...] = topw / jnp.sum(topw, axis=1, keepdims=True)
    cnt_ref[0] = jnp.sum(chosen, axis=0, keepdims=True)


def _mix(x2, c, o, gates, wc, wa, wo, g2, rw, rb, *, tm):
    t, d = x2.shape
    row = lambda n: pl.BlockSpec((tm, n), lambda i: (i, 0))
    return pl.pallas_call(
        functools.partial(_mix_kernel, d=d, tm=tm),
        grid=(t // tm,),
        in_specs=[row(d), row(d), row(d), row(2 * d), _resident((d, d)), _resident((d, d)), _resident((d, d)),
                  _resident((1, d)), _resident((d, 2 * LANES)), _resident((1, LANES))],
        out_specs=[row(d), pl.BlockSpec((tm * ROW_SLAB, LANES), lambda i: (i, 0)), row(LANES), row(LANES),
                   pl.BlockSpec((1, 1, LANES), lambda i: (i, 0, 0))],
        out_shape=[jax.ShapeDtypeStruct((t, d), F32), jax.ShapeDtypeStruct((t * ROW_SLAB, LANES), jnp.uint32),
                   jax.ShapeDtypeStruct((t, LANES), jnp.int32), jax.ShapeDtypeStruct((t, LANES), F32),
                   jax.ShapeDtypeStruct((t // tm, 1, LANES), F32)],
        compiler_params=_params(("parallel",)),
        name="mix_router",
    )(x2, c, o, gates, wc, wa, wo, g2, rw, rb)


def _plan_kernel(topi_ref, base_ref, dest_ref, *, tt):
    idx = topi_ref[...]
    lane = lax.broadcasted_iota(jnp.int32, idx.shape, 1)
    onehots = [lane == idx[:, k:k + 1] for k in range(TOP_K)]
    chosen = jnp.zeros(idx.shape, F32)
    for oh in onehots:
        chosen = jnp.where(oh, 1.0, chosen)
    r_iota = lax.broadcasted_iota(jnp.int32, (tt, tt), 0)
    c_iota = lax.broadcasted_iota(jnp.int32, (tt, tt), 1)
    earlier = jnp.where(c_iota < r_iota, 1.0, 0.0).astype(BF16)
    rank = jnp.dot(earlier, chosen.astype(BF16), preferred_element_type=F32)
    pos = rank + base_ref[0]
    dest = jnp.zeros(idx.shape, F32)
    for k, oh in enumerate(onehots):
        dk = jnp.sum(jnp.where(oh, pos, 0.0), axis=1, keepdims=True)
        dest = jnp.where(lane == k, dk, dest)
    dest_ref[...] = dest.T[0:SLAB, :].astype(jnp.int32)


def _plan(topi, base, *, tt):
    t = topi.shape[0]
    nt = t // tt
    return pl.pallas_call(
        functools.partial(_plan_kernel, tt=tt),
        grid=(nt,),
        in_specs=[pl.BlockSpec((tt, LANES), lambda i: (i, 0)), pl.BlockSpec((1, 1, LANES), lambda i: (i, 0, 0))],
        out_specs=pl.BlockSpec((SLAB, tt), lambda i: (i, 0)),
        out_shape=jax.ShapeDtypeStruct((nt * SLAB, tt), jnp.int32),
        compiler_params=_params(("parallel",)),
        name="moe_plan",
    )(topi, base)


def _row_copy(src_ref, src_row, dst_ref, dst_row, sem):
    return pltpu.make_async_copy(src_ref.at[pl.ds(pl.multiple_of(src_row * ROW_SLAB, ROW_SLAB), ROW_SLAB)],
                                 dst_ref.at[pl.ds(pl.multiple_of(dst_row * ROW_SLAB, ROW_SLAB), ROW_SLAB)], sem)


def _dest_load(dest_hbm, dest_smem, isem, tile, slot, tt):
    n = SLAB * tt
    return pltpu.make_async_copy(dest_hbm.at[pl.ds(pl.multiple_of(tile * n, n), n)],
                                 dest_smem.at[pl.ds(pl.multiple_of(slot * n, n), n)], isem)


def _pad_fill_copies(pad_start_ref, pad_cnt_ref, zero_ref, xs_hbm, sem, e, tr):
    cnt = pad_cnt_ref[e]
    out = []
    bit = 1
    while bit < tr:
        first = pad_start_ref[e] + (cnt & (bit - 1))
        copy = pltpu.make_async_copy(zero_ref.at[pl.ds(0, bit * ROW_SLAB)],
                                     xs_hbm.at[pl.ds(pl.multiple_of(first * ROW_SLAB, ROW_SLAB), bit * ROW_SLAB)], sem)
        out.append(((cnt & bit) != 0, copy))
        bit *= 2
    return out


def _dispatch_kernel(pad_start_ref, pad_cnt_ref, dest_hbm, h2s_ref, xs_hbm, dest_smem, zero_ref, sem, isem, zsem,
                     *, tt, tr, n_e):
    i = pl.program_id(0)
    slot = i % 2

    @pl.when(i == 0)
    def _():
        _dest_load(dest_hbm, dest_smem, isem, 0, 0, tt).start()

    @pl.when(i == 0)
    def _():
        zero_ref[...] = jnp.zeros_like(zero_ref)
        for e in range(n_e):
            for pred, copy in _pad_fill_copies(pad_start_ref, pad_cnt_ref, zero_ref, xs_hbm, zsem, e, tr):
                pl.when(pred)(copy.start)

    _dest_load(dest_hbm, dest_smem, isem, i, slot, tt).wait()

    @pl.when(i + 1 < pl.num_programs(0))
    def _():
        _dest_load(dest_hbm, dest_smem, isem, i + 1, 1 - slot, tt).start()

    def issue(g, carry):
        toks = [g * ISSUE_GROUP + j for j in range(ISSUE_GROUP)]
        rows = [[dest_smem[(slot * SLAB + k) * tt + tok] for k in range(TOP_K)] for tok in toks]
        for m, (tok, row) in enumerate(zip(toks, rows)):
            for k in range(TOP_K):
                _row_copy(h2s_ref, tok, xs_hbm, row[k], sem).start(priority=(m * TOP_K + k) % 2)
        return carry
    lax.fori_loop(0, tt // ISSUE_GROUP, issue, 0)

    def drain(tok, carry):
        for k in range(TOP_K):
            _row_copy(h2s_ref, 0, xs_hbm, 0, sem).wait()
        return carry
    lax.fori_loop(0, tt, drain, 0, unroll=8)

    @pl.when(i == pl.num_programs(0) - 1)
    def _():
        for e in range(n_e):
            for pred, copy in _pad_fill_copies(pad_start_ref, pad_cnt_ref, zero_ref, xs_hbm, zsem, e, tr):
                pl.when(pred)(copy.wait)


def _dispatch(pad_start, pad_cnt, dest, h2s, *, rows, tt, tr):
    nt = dest.shape[0] // (SLAB * tt)
    n_e = pad_start.shape[0]
    anyspec = pl.BlockSpec(memory_space=pl.ANY)
    return pl.pallas_call(
        functools.partial(_dispatch_kernel, tt=tt, tr=tr, n_e=n_e),
        grid_spec=pltpu.PrefetchScalarGridSpec(
            num_scalar_prefetch=2, grid=(nt,),
            in_specs=[anyspec, pl.BlockSpec((tt * ROW_SLAB, LANES), lambda i, ps, pc: (i, 0))], out_specs=anyspec,
            scratch_shapes=[pltpu.SMEM((2 * SLAB * tt,), jnp.int32), pltpu.VMEM((tr // 2 * ROW_SLAB, LANES), jnp.uint32),
                            pltpu.SemaphoreType.DMA, pltpu.SemaphoreType.DMA, pltpu.SemaphoreType.DMA]),
        out_shape=jax.ShapeDtypeStruct((rows * ROW_SLAB, LANES), jnp.uint32),
        compiler_params=_params(("arbitrary",)),
        name="moe_dispatch",
    )(pad_start, pad_cnt, dest, h2s)


def _ffn_kernel(te_ref, nu_ref, xs_ref, w1_ref, b1_ref, w2_ref, b2_ref, o_ref, w1b_ref, w2b_ref, *, tr, dff):
    r = pl.program_id(0)

    @pl.when(jnp.logical_or(r == 0, te_ref[r] != te_ref[jnp.maximum(r - 1, 0)]))
    def _():
        w1b_ref[...] = w1_ref[0].astype(BF16)
        w2b_ref[...] = w2_ref[0].astype(BF16)

    @pl.when(r < nu_ref[0])
    def _():
        xs = _unpack_rows(xs_ref, tr).astype(BF16)
        hid = jnp.dot(xs, w1b_ref[...], preferred_element_type=F32) + b1_ref[0]
        glu = jnp.minimum(hid[:, :dff], SWIGLU_LIMIT)
        lin = jnp.clip(hid[:, dff:], -SWIGLU_LIMIT, SWIGLU_LIMIT)
        act = glu * _sigmoid(SWIGLU_ALPHA * glu) * (lin + 1.0)
        out = jnp.dot(act.astype(BF16), w2b_ref[...], preferred_element_type=F32) + b2_ref[0]
        _pack_rows(o_ref, out, tr)


def _ffn(tile_expert, n_used, xs, w1, b1, w2, b2, *, tr):
    n_e, d, dff2 = w1.shape
    n_tiles = tile_expert.shape[0]
    rows = lambda r, te, nu: (jnp.minimum(r, nu[0] - 1), 0)
    per_e = lambda a, b: pl.BlockSpec((1, a, b), lambda r, te, nu: (te[r], 0, 0))
    return pl.pallas_call(
        functools.partial(_ffn_kernel, tr=tr, dff=dff2 // 2),
        grid_spec=pltpu.PrefetchScalarGridSpec(
            num_scalar_prefetch=2, grid=(n_tiles,),
            in_specs=[pl.BlockSpec((tr * ROW_SLAB, LANES), rows), per_e(d, dff2), per_e(1, dff2), per_e(dff2 // 2, d),
                      per_e(1, d)],
            out_specs=pl.BlockSpec((tr * ROW_SLAB, LANES), rows),
            scratch_shapes=[pltpu.VMEM((d, dff2), BF16), pltpu.VMEM((dff2 // 2, d), BF16)]),
        out_shape=jax.ShapeDtypeStruct(xs.shape, jnp.uint32),
        compiler_params=_params(("arbitrary",)),
        name="moe_experts",
    )(tile_expert, n_used, xs, w1, b1, w2, b2)


def _combine_kernel(dest_hbm, ys_hbm, x1_ref, topw_ref, fg_ref, o_ref, dest_smem, buf_ref, sems, isem, *, tt):
    i = pl.program_id(0)
    n = pl.num_programs(0)

    def gather(tile):
        slot = tile % 2
        _dest_load(dest_hbm, dest_smem, isem, tile, slot, tt).wait()

        @pl.when(tile + 1 < n)
        def _():
            _dest_load(dest_hbm, dest_smem, isem, tile + 1, 1 - slot, tt).start()

        def issue(g, carry):
            toks = [g * ISSUE_GROUP + j for j in range(ISSUE_GROUP)]
            rows = [[dest_smem[(slot * SLAB + k) * tt + tok] for k in range(TOP_K)] for tok in toks]
            for m, (tok, row) in enumerate(zip(toks, rows)):
                for k in range(TOP_K):
                    _row_copy(ys_hbm, row[k], buf_ref.at[slot, k], tok, sems.at[slot]).start(
                        priority=(m * TOP_K + k) % 2)
            return carry
        lax.fori_loop(0, tt // ISSUE_GROUP, issue, 0)

    @pl.when(i == 0)
    def _():
        _dest_load(dest_hbm, dest_smem, isem, 0, 0, tt).start()
        gather(0)

    @pl.when(i + 1 < n)
    def _():
        gather(i + 1)

    slot = i % 2

    def drain(tok, carry):
        for k in range(TOP_K):
            _row_copy(ys_hbm, 0, buf_ref.at[slot, k], 0, sems.at[slot]).wait()
        return carry
    lax.fori_loop(0, tt, drain, 0, unroll=8)

    y = x1_ref[...]
    w = topw_ref[...]
    for k in range(TOP_K):
        y = y + w[:, k:k + 1] * _unpack_rows(buf_ref, tt, lead=(slot, k))
    ms = jnp.mean(y * y, axis=-1, keepdims=True)
    o_ref[...] = y * lax.rsqrt(ms + EPS) * fg_ref[...]


def _combine(dest, ys, x1, topw, fg, *, tt):
    t, d = x1.shape
    row = lambda n: pl.BlockSpec((tt, n), lambda i: (i, 0))
    anyspec = pl.BlockSpec(memory_space=pl.ANY)
    return pl.pallas_call(
        functools.partial(_combine_kernel, tt=tt),
        grid=(t // tt,),
        in_specs=[anyspec, anyspec, row(d), row(LANES), _resident((1, d))],
        out_specs=row(d),
        out_shape=jax.ShapeDtypeStruct((t, d), F32),
        scratch_shapes=[pltpu.SMEM((2 * SLAB * tt,), jnp.int32), pltpu.VMEM((2, TOP_K, tt * ROW_SLAB, LANES), jnp.uint32),
                        pltpu.SemaphoreType.DMA((2,)), pltpu.SemaphoreType.DMA],
        compiler_params=_params(("arbitrary",)),
        name="moe_combine",
    )(dest, ys, x1, topw, fg)


def _moe(x1, h2s, topi, topw, counts, w1, b1, w2, b2, fg, *, tt, tr):
    t = x1.shape[0]
    n_e = w1.shape[0]
    cnt = counts[:, 0, :n_e].astype(jnp.int32)
    nt = cnt.shape[0]
    total = jnp.sum(cnt, axis=0)
    padded = (total + tr - 1) // tr * tr
    e_ids = jnp.arange(n_e)
    ends = jnp.sum(jnp.where(e_ids[:, None] <= e_ids[None, :], padded[:, None], 0), axis=0)
    starts = ends - padded
    t_ids = jnp.arange(nt)
    before = jnp.sum(jnp.where((t_ids[:, None] < t_ids[None, :])[:, :, None], cnt[:, None, :], 0), axis=0)
    base = starts[None, :] + before
    base = jnp.pad(base.astype(F32), ((0, 0), (0, LANES - n_e)))[:, None, :]
    n_tiles = (t * TOP_K) // tr + n_e
    n_used = (ends[-1] // tr).astype(jnp.int32)
    r = jnp.minimum(jnp.arange(n_tiles, dtype=jnp.int32), n_used - 1)
    tile_expert = jnp.sum((ends // tr)[None, :] <= r[:, None], axis=1).astype(jnp.int32)

    dest = _plan(topi, base, tt=tt).reshape(-1)
    xs = _dispatch(starts + total, padded - total, dest, h2s, rows=n_tiles * tr, tt=tt, tr=tr)
    ys = _ffn(tile_expert, n_used.reshape(1), xs, w1, b1, w2, b2, tr=tr)
    return _combine(dest, ys, x1, topw, fg, tt=tt)


class _Tiles(NamedTuple):
    tokens: int
    attention: int
    conv: int
    expert_rows: int


def _tile(n, want):
    want = min(want, n)
    assert n % want == 0, (n, want)
    return want


def _choose_tiles(batch, seq):
    return _Tiles(tokens=_tile(batch * seq, 512), attention=_tile(seq, 256), conv=_tile(seq, 256),
                  expert_rows=_tile(batch * seq * TOP_K, 512))


def kernel(x, norm1_g, w_in, conv_dw_w, conv_dw_b, conv_norm_g, conv_norm_b, w_conv_out, w_attn_out, gate_b,
           w_out, norm2_g, router_w, router_b, expert_w1, expert_b1, expert_w2, expert_b2, final_norm_g):
    batch, seq, d = x.shape
    depth = w_in.shape[0]
    t = batch * seq
    assert d == 2 * ROW_SLAB * LANES and N_EXPERTS <= LANES
    assert depth == 1, "the final rms_norm is fused into the combine kernel of the only layer"
    row2 = lambda a: a.reshape(1, -1)
    l = 0

    tiles = _choose_tiles(batch, seq)
    x2 = x.reshape(t, d)
    u, q, k, v, gates = _in_proj(x2, row2(norm1_g[l]), w_in[l].astype(BF16), row2(gate_b[l]), tm=tiles.tokens)
    o = _attention(q, k, v, batch=batch, seq=seq, tb=tiles.attention, heads=ATTN_HEADS_PER_STEP)
    c = _conv_branch(u, conv_dw_w[l], row2(conv_dw_b[l]), row2(conv_norm_g[l]), row2(conv_norm_b[l]),
                     batch=batch, seq=seq, ts=tiles.conv)
    rw = jnp.pad(router_w[l], ((0, 0), (0, LANES - N_EXPERTS)))
    rw_hi = rw.astype(BF16)
    rw_lo = (rw - rw_hi.astype(F32)).astype(BF16)
    rb = jnp.pad(router_b[l].astype(F32), (0, LANES - N_EXPERTS), constant_values=-1e30).reshape(1, LANES)
    x1, h2s, topi, topw, counts = _mix(x2, c, o, gates, w_conv_out[l].astype(BF16), w_attn_out[l].astype(BF16),
                                       w_out[l].astype(BF16), row2(norm2_g[l]),
                                       jnp.concatenate([rw_hi, rw_lo], axis=1), rb, tm=tiles.tokens)
    out = _moe(x1, h2s, topi, topw, counts, expert_w1[l], expert_b1[l][:, None, :],
               expert_w2[l], expert_b2[l][:, None, :], row2(final_norm_g), tt=tiles.tokens, tr=tiles.expert_rows)
    return out.reshape(batch, seq, d)
```

```python
import functools
from typing import NamedTuple

import jax
import jax.numpy as jnp
from jax import lax
from jax.experimental import pallas as pl
from jax.experimental.pallas import tpu as pltpu

F32 = jnp.float32
BF16 = jnp.bfloat16

HEAD_DIM = 64
CONV_WIDTH = 31
N_EXPERTS = 32
TOP_K = 4
SWIGLU_LIMIT = 7.0
SWIGLU_ALPHA = 1.702
EPS = 1e-6

LANES = 128
SLAB = 8
ROW_SLAB = 4
HIGH_HALF = 0xFFFF0000
ATTN_HEADS_PER_STEP = 4
ATTN_TILES_PER_STEP = 2
ISSUE_GROUP = 8
CONV_HALO = 32
CONV_ROWS = 128
VMEM_LIMIT = 56 * 1024 * 1024

LOG2_E = 1.4426950408889634
SKIP_THRESHOLD = 150.0 * (1.0 + 2.0 ** -7)
SIGN_BIT = 0x80000000


def _params(sem):
    return pltpu.CompilerParams(dimension_semantics=sem, vmem_limit_bytes=VMEM_LIMIT)


def _resident(shape):
    return pl.BlockSpec(shape, lambda *_: (0,) * len(shape), pipeline_mode=pl.Buffered(1))


def _sigmoid(x):
    return 1.0 / (1.0 + jnp.exp(-x))


def _inproj_kernel(x_ref, g_ref, w_ref, gb_ref, u_ref, q_ref, k_ref, v_ref, gate_ref, *, d):
    x = x_ref[...]
    ms = jnp.mean(x * x, axis=-1, keepdims=True)
    h = (x * lax.rsqrt(ms + EPS) * g_ref[...]).astype(BF16)

    def proj(c):
        return jnp.dot(h, w_ref[:, c * d:(c + 1) * d], preferred_element_type=F32)

    u_ref[...] = (proj(0) * _sigmoid(proj(1))).astype(BF16)
    q_ref[...] = (proj(2) * (HEAD_DIM ** -0.5 * LOG2_E)).astype(BF16)
    k_ref[...] = proj(3).astype(BF16)
    v_ref[...] = proj(4).astype(BF16)
    gate_ref[:, :d] = _sigmoid(proj(5) + gb_ref[:, :d]).astype(BF16)
    gate_ref[:, d:] = _sigmoid(proj(6) + gb_ref[:, d:]).astype(BF16)


def _in_proj(x2, norm_g, w_in, gate_b, *, tm):
    t, d = x2.shape
    row = lambda n: pl.BlockSpec((tm, n), lambda i: (i, 0))
    out = lambda n: jax.ShapeDtypeStruct((t, n), BF16)
    return pl.pallas_call(
        functools.partial(_inproj_kernel, d=d),
        grid=(t // tm,),
        in_specs=[row(d), _resident((1, d)), _resident(w_in.shape), _resident((1, 2 * d))],
        out_specs=[row(d), row(d), row(d), row(d), row(2 * d)],
        out_shape=[out(d), out(d), out(d), out(d), out(2 * d)],
        compiler_params=_params(("parallel",)),
        name="in_proj",
    )(x2, norm_g, w_in, gate_b)


def _attn_tile(q_ref, k_ref, v_ref, o_ref, acc_ref, mass_ref, i, rows, *, tb, heads):
    r_iota = lax.broadcasted_iota(jnp.int32, (tb, tb), 0)
    c_iota = lax.broadcasted_iota(jnp.int32, (tb, tb), 1)
    upper = jnp.where(r_iota > c_iota, 1.0, 0.0).astype(BF16)
    causal = c_iota < r_iota

    first_head = lax.broadcasted_iota(jnp.int32, (tb, LANES), 1) < HEAD_DIM
    zeros = jnp.zeros((tb, LANES), BF16)

    def split_heads(x2):
        return jnp.concatenate([jnp.where(first_head, x2, zeros), jnp.where(first_head, zeros, x2)], axis=0)

    def pair_block(p, j, carry, acc, diagonal):
        lanes = slice(p * LANES, (p + 1) * LANES)
        start = pl.multiple_of(j * tb, tb)
        z = lax.dot_general(q_ref[rows, lanes], split_heads(k_ref[pl.ds(start, tb), lanes]),
                            (((1,), (1,)), ((), ())), preferred_element_type=F32)
        neg_abs = pltpu.bitcast(pltpu.bitcast(z, jnp.uint32) | jnp.uint32(SIGN_BIT), F32)
        l = jnp.log2(1.0 + jnp.exp2(neg_abs))
        sp = jnp.maximum(z, 0.0) + l
        logsig = z - sp
        if diagonal:
            sp = jnp.where(causal2, sp, 0.0)
        spb = sp.astype(BF16)
        prefix = jnp.dot(jnp.concatenate([spb[:, :tb], spb[:, tb:]], axis=0), upper,
                         preferred_element_type=F32)
        mass = (prefix[0:tb, 0:1] + sp[:, 0:1], prefix[tb:, 0:1] + sp[:, tb:tb + 1])
        later = jnp.concatenate([prefix[0:tb] + carry[0], prefix[tb:] + carry[1]], axis=1)
        a = jnp.exp2(logsig - later)
        if diagonal:
            a = jnp.where(causal2, a, 0.0)
        acc = acc + jnp.dot(a.astype(BF16), split_heads(v_ref[pl.ds(start, tb), lanes]),
                            preferred_element_type=F32)
        return (carry[0] + mass[0], carry[1] + mass[1]), acc

    pairs = heads * HEAD_DIM // LANES
    causal2 = jnp.concatenate([causal, causal], axis=1)

    def blocks(j, carries, accs, diagonal):
        new = [pair_block(p, j, carries[p], accs[p], diagonal) for p in range(pairs)]
        return tuple(c for c, _ in new), tuple(a for _, a in new)

    zero_carry = (jnp.zeros((tb, 1), F32), jnp.zeros((tb, 1), F32))

    def least_mass(carries):
        return jnp.min(functools.reduce(jnp.minimum, [c for pair in carries for c in pair]))

    def publish(carries, accs):
        for p in range(pairs):
            o_ref[rows, p * LANES:(p + 1) * LANES] = accs[p].astype(BF16)
            acc_ref[:, p * LANES:(p + 1) * LANES] = accs[p]
            for h in range(2):
                mass_ref[:, 2 * p + h:2 * p + h + 1] = carries[p][h]
        return least_mass(carries)

    def leading_blocks(with_previous):
        carries, accs = blocks(i, (zero_carry,) * pairs, (jnp.zeros((tb, LANES), F32),) * pairs, True)
        if with_previous:
            carries, accs = blocks(i - 1, carries, accs, False)
        return publish(carries, accs)

    least = lax.cond(i > 0, lambda: leading_blocks(True), lambda: leading_blocks(False))

    def cond(state):
        j, least = state
        return jnp.logical_and(j >= 0, least < SKIP_THRESHOLD)

    def body(state):
        j, _ = state
        carries = tuple((mass_ref[:, 2 * p:2 * p + 1], mass_ref[:, 2 * p + 1:2 * p + 2]) for p in range(pairs))
        accs = tuple(acc_ref[:, p * LANES:(p + 1) * LANES] for p in range(pairs))
        return j - 1, publish(*blocks(j, carries, accs, False))

    lax.while_loop(cond, body, (i - 2, least))


def _attn_kernel(q_ref, k_ref, v_ref, o_ref, acc_ref, mass_ref, *, tb, heads, tiles):
    step = pl.program_id(2)
    for n in range(tiles):
        _attn_tile(q_ref, k_ref, v_ref, o_ref, acc_ref, mass_ref, step * tiles + n, slice(n * tb, (n + 1) * tb),
                   tb=tb, heads=heads)


def _attention(q, k, v, *, batch, seq, tb, heads):
    t, d = q.shape
    tiles = _tile(seq // tb, ATTN_TILES_PER_STEP)
    nq = seq // (tb * tiles)
    width = heads * HEAD_DIM
    qspec = pl.BlockSpec((tb * tiles, width), lambda b, hp, i: (b * nq + i, hp))
    kvspec = pl.BlockSpec((seq, width), lambda b, hp, i: (b, hp))
    return pl.pallas_call(
        functools.partial(_attn_kernel, tb=tb, heads=heads, tiles=tiles),
        grid=(batch, d // width, nq),
        in_specs=[qspec, kvspec, kvspec],
        out_specs=qspec,
        out_shape=jax.ShapeDtypeStruct((t, d), BF16),
        scratch_shapes=[pltpu.VMEM((tb, width), F32), pltpu.VMEM((tb, LANES), F32)],
        compiler_params=_params(("parallel", "parallel", "arbitrary")),
        name="attention",
    )(q, k, v)


def _conv_kernel(prev_ref, cur_ref, w_ref, b_ref, g_ref, nb_ref, o_ref, win_ref, conv_ref, *, ts, d):
    i = pl.program_id(1)
    prev = prev_ref[...].astype(F32)
    win_ref[0:CONV_HALO, :] = jnp.where(i == 0, 0.0, prev)
    win_ref[CONV_HALO:, :] = cur_ref[...].astype(F32)
    base = CONV_HALO - (CONV_WIDTH - 1)
    rows = CONV_ROWS
    span = rows + CONV_HALO
    for c in range(d // LANES):
        lanes = slice(c * LANES, (c + 1) * LANES)
        for r0 in range(0, ts, rows):
            window = win_ref[r0:r0 + span, lanes]
            acc = jnp.zeros((rows, LANES), F32)
            for shift in range(SLAB):
                shifted = pltpu.roll(window, (span - shift) % span, axis=0) if shift else window
                for w in range(CONV_WIDTH):
                    if (base + w) % SLAB == shift:
                        lead = (base + w) // SLAB * SLAB
                        acc = acc + shifted[lead:lead + rows] * w_ref[w:w + 1, lanes]
            conv_ref[r0:r0 + rows, lanes] = acc + b_ref[:, lanes]
    y = conv_ref[...]
    mu = jnp.mean(y, axis=-1, keepdims=True)
    yc = y - mu
    var = jnp.mean(yc * yc, axis=-1, keepdims=True)
    yn = yc * lax.rsqrt(var + EPS) * g_ref[...] + nb_ref[...]
    o_ref[...] = (yn * _sigmoid(yn)).astype(BF16)


def _conv_branch(u, dw_w, dw_b, norm_g, norm_b, *, batch, seq, ts):
    t, d = u.shape
    ns = seq // ts
    per = ts // CONV_HALO
    cur = pl.BlockSpec((ts, d), lambda b, i: (b * ns + i, 0))
    prev = pl.BlockSpec((CONV_HALO, d), lambda b, i: (jnp.maximum((b * ns + i) * per - 1, 0), 0))
    return pl.pallas_call(
        functools.partial(_conv_kernel, ts=ts, d=d),
        grid=(batch, ns),
        in_specs=[prev, cur, _resident(dw_w.shape), _resident((1, d)), _resident((1, d)), _resident((1, d))],
        out_specs=cur,
        out_shape=jax.ShapeDtypeStruct((t, d), BF16),
        scratch_shapes=[pltpu.VMEM((ts + CONV_HALO, d), F32), pltpu.VMEM((ts, d), F32)],
        compiler_params=_params(("parallel", "parallel")),
        name="conv_branch",
    )(u, u, dw_w, dw_b, norm_g, norm_b)


def _pack_rows(ref, val, rows):
    half = val.shape[1] // 2
    rounded = val.astype(BF16).astype(F32)
    words = (lax.shift_right_logical(pltpu.bitcast(rounded[:, :half], jnp.uint32), jnp.uint32(16))
             | (pltpu.bitcast(rounded[:, half:], jnp.uint32) & jnp.uint32(HIGH_HALF)))
    for c in range(ROW_SLAB):
        ref[pl.ds(c, rows, stride=ROW_SLAB), :] = words[:, c * LANES:(c + 1) * LANES]


def _unpack_rows(ref, rows, lead=()):
    words = [ref[lead + (pl.ds(c, rows, stride=ROW_SLAB), slice(None))] for c in range(ROW_SLAB)]
    low = [pltpu.bitcast(lax.shift_left(w, jnp.uint32(16)), F32) for w in words]
    high = [pltpu.bitcast(w & jnp.uint32(HIGH_HALF), F32) for w in words]
    return jnp.concatenate(low + high, axis=1)


def _mix_kernel(x_ref, c_ref, o_ref, gate_ref, wc_ref, wa_ref, wo_ref, g2_ref, rw_ref, rb_ref,
                x1_ref, h2s_ref, topi_ref, topw_ref, cnt_ref, *, d, tm):
    y_conv = jnp.dot(c_ref[...], wc_ref[...], preferred_element_type=F32)
    y_attn = jnp.dot(o_ref[...], wa_ref[...], preferred_element_type=F32)
    m = gate_ref[:, :d].astype(F32) * y_conv + gate_ref[:, d:].astype(F32) * y_attn
    x1 = x_ref[...] + jnp.dot(m.astype(BF16), wo_ref[...], preferred_element_type=F32)
    x1_ref[...] = x1
    ms = jnp.mean(x1 * x1, axis=-1, keepdims=True)
    h2 = x1 * lax.rsqrt(ms + EPS) * g2_ref[...]
    h2_hi = h2.astype(BF16)
    _pack_rows(h2s_ref, h2, tm)
    h2_lo = (h2 - h2_hi.astype(F32)).astype(BF16)
    both = jnp.dot(h2_hi, rw_ref[...], preferred_element_type=F32)
    logits = (both[:, :LANES] + both[:, LANES:]
              + jnp.dot(h2_lo, rw_ref[:, :LANES], preferred_element_type=F32)
              + rb_ref[...])
    lane = lax.broadcasted_iota(jnp.int32, logits.shape, 1)
    work = logits
    top = None
    topi = jnp.zeros(logits.shape, jnp.int32)
    topw = jnp.zeros_like(logits)
    chosen = jnp.zeros_like(logits)
    for k in range(TOP_K):
        mx = jnp.max(work, axis=1, keepdims=True)
        idx = jnp.min(jnp.where(work == mx, lane, LANES), axis=1, keepdims=True)
        sel = lane == idx
        if k == 0:
            top = mx
        topi = jnp.where(lane == k, idx, topi)
        topw = jnp.where(lane == k, jnp.exp(mx - top), topw)
        chosen = jnp.where(sel, 1.0, chosen)
        work = jnp.where(sel, -jnp.inf, work)
    topi_ref[...] = topi
    topw_ref[...] = topw / jnp.sum(topw, axis=1, keepdims=True)
    cnt_ref[0] = jnp.sum(chosen, axis=0, keepdims=True)


def _mix(x2, c, o, gates, wc, wa, wo, g2, rw, rb, *, tm):
    t, d = x2.shape
    row = lambda n: pl.BlockSpec((tm, n), lambda i: (i, 0))
    return pl.pallas_call(
        functools.partial(_mix_kernel, d=d, tm=tm),
        grid=(t // tm,),
        in_specs=[row(d), row(d), row(d), row(2 * d), _resident((d, d)), _resident((d, d)), _resident((d, d)),
                  _resident((1, d)), _resident((d, 2 * LANES)), _resident((1, LANES))],
        out_specs=[row(d), pl.BlockSpec((tm * ROW_SLAB, LANES), lambda i: (i, 0)), row(LANES), row(LANES),
                   pl.BlockSpec((1, 1, LANES), lambda i: (i, 0, 0))],
        out_shape=[jax.ShapeDtypeStruct((t, d), F32), jax.ShapeDtypeStruct((t * ROW_SLAB, LANES), jnp.uint32),
                   jax.ShapeDtypeStruct((t, LANES), jnp.int32), jax.ShapeDtypeStruct((t, LANES), F32),
                   jax.ShapeDtypeStruct((t // tm, 1, LANES), F32)],
        compiler_params=_params(("parallel",)),
        name="mix_router",
    )(x2, c, o, gates, wc, wa, wo, g2, rw, rb)


def _plan_kernel(topi_ref, base_ref, dest_ref, *, tt):
    idx = topi_ref[...]
    lane = lax.broadcasted_iota(jnp.int32, idx.shape, 1)
    onehots = [lane == idx[:, k:k + 1] for k in range(TOP_K)]
    chosen = jnp.zeros(idx.shape, F32)
    for oh in onehots:
        chosen = jnp.where(oh, 1.0, chosen)
    r_iota = lax.broadcasted_iota(jnp.int32, (tt, tt), 0)
    c_iota = lax.broadcasted_iota(jnp.int32, (tt, tt), 1)
    earlier = jnp.where(c_iota < r_iota, 1.0, 0.0).astype(BF16)
    rank = jnp.dot(earlier, chosen.astype(BF16), preferred_element_type=F32)
    pos = rank + base_ref[0]
    dest = jnp.zeros(idx.shape, F32)
    for k, oh in enumerate(onehots):
        dk = jnp.sum(jnp.where(oh, pos, 0.0), axis=1, keepdims=True)
        dest = jnp.where(lane == k, dk, dest)
    dest_ref[...] = dest.T[0:SLAB, :].astype(jnp.int32)


def _plan(topi, base, *, tt):
    t = topi.shape[0]
    nt = t // tt
    return pl.pallas_call(
        functools.partial(_plan_kernel, tt=tt),
        grid=(nt,),
        in_specs=[pl.BlockSpec((tt, LANES), lambda i: (i, 0)), pl.BlockSpec((1, 1, LANES), lambda i: (i, 0, 0))],
        out_specs=pl.BlockSpec((SLAB, tt), lambda i: (i, 0)),
        out_shape=jax.ShapeDtypeStruct((nt * SLAB, tt), jnp.int32),
        compiler_params=_params(("parallel",)),
        name="moe_plan",
    )(topi, base)


def _row_copy(src_ref, src_row, dst_ref, dst_row, sem):
    return pltpu.make_async_copy(src_ref.at[pl.ds(pl.multiple_of(src_row * ROW_SLAB, ROW_SLAB), ROW_SLAB)],
                                 dst_ref.at[pl.ds(pl.multiple_of(dst_row * ROW_SLAB, ROW_SLAB), ROW_SLAB)], sem)


def _dest_load(dest_hbm, dest_smem, isem, tile, slot, tt):
    n = SLAB * tt
    return pltpu.make_async_copy(dest_hbm.at[pl.ds(pl.multiple_of(tile * n, n), n)],
                                 dest_smem.at[pl.ds(pl.multiple_of(slot * n, n), n)], isem)


def _pad_fill_copies(pad_start_ref, pad_cnt_ref, zero_ref, xs_hbm, sem, e, tr):
    cnt = pad_cnt_ref[e]
    out = []
    bit = 1
    while bit < tr:
        first = pad_start_ref[e] + (cnt & (bit - 1))
        copy = pltpu.make_async_copy(zero_ref.at[pl.ds(0, bit * ROW_SLAB)],
                                     xs_hbm.at[pl.ds(pl.multiple_of(first * ROW_SLAB, ROW_SLAB), bit * ROW_SLAB)], sem)
        out.append(((cnt & bit) != 0, copy))
        bit *= 2
    return out


def _dispatch_kernel(pad_start_ref, pad_cnt_ref, dest_hbm, h2s_ref, xs_hbm, dest_smem, zero_ref, sem, isem, zsem,
                     *, tt, tr, n_e):
    i = pl.program_id(0)
    slot = i % 2

    @pl.when(i == 0)
    def _():
        _dest_load(dest_hbm, dest_smem, isem, 0, 0, tt).start()

    @pl.when(i == 0)
    def _():
        zero_ref[...] = jnp.zeros_like(zero_ref)
        for e in range(n_e):
            for pred, copy in _pad_fill_copies(pad_start_ref, pad_cnt_ref, zero_ref, xs_hbm, zsem, e, tr):
                pl.when(pred)(copy.start)

    _dest_load(dest_hbm, dest_smem, isem, i, slot, tt).wait()

    @pl.when(i + 1 < pl.num_programs(0))
    def _():
        _dest_load(dest_hbm, dest_smem, isem, i + 1, 1 - slot, tt).start()

    def issue(g, carry):
        toks = [g * ISSUE_GROUP + j for j in range(ISSUE_GROUP)]
        rows = [[dest_smem[(slot * SLAB + k) * tt + tok] for k in range(TOP_K)] for tok in toks]
        for m, (tok, row) in enumerate(zip(toks, rows)):
            for k in range(TOP_K):
                _row_copy(h2s_ref, tok, xs_hbm, row[k], sem).start(priority=(m * TOP_K + k) % 2)
        return carry
    lax.fori_loop(0, tt // ISSUE_GROUP, issue, 0)

    def drain(tok, carry):
        for k in range(TOP_K):
            _row_copy(h2s_ref, 0, xs_hbm, 0, sem).wait()
        return carry
    lax.fori_loop(0, tt, drain, 0, unroll=8)

    @pl.when(i == pl.num_programs(0) - 1)
    def _():
        for e in range(n_e):
            for pred, copy in _pad_fill_copies(pad_start_ref, pad_cnt_ref, zero_ref, xs_hbm, zsem, e, tr):
                pl.when(pred)(copy.wait)


def _dispatch(pad_start, pad_cnt, dest, h2s, *, rows, tt, tr):
    nt = dest.shape[0] // (SLAB * tt)
    n_e = pad_start.shape[0]
    anyspec = pl.BlockSpec(memory_space=pl.ANY)
    return pl.pallas_call(
        functools.partial(_dispatch_kernel, tt=tt, tr=tr, n_e=n_e),
        grid_spec=pltpu.PrefetchScalarGridSpec(
            num_scalar_prefetch=2, grid=(nt,),
            in_specs=[anyspec, pl.BlockSpec((tt * ROW_SLAB, LANES), lambda i, ps, pc: (i, 0))], out_specs=anyspec,
            scratch_shapes=[pltpu.SMEM((2 * SLAB * tt,), jnp.int32), pltpu.VMEM((tr // 2 * ROW_SLAB, LANES), jnp.uint32),
                            pltpu.SemaphoreType.DMA, pltpu.SemaphoreType.DMA, pltpu.SemaphoreType.DMA]),
        out_shape=jax.ShapeDtypeStruct((rows * ROW_SLAB, LANES), jnp.uint32),
        compiler_params=_params(("arbitrary",)),
        name="moe_dispatch",
    )(pad_start, pad_cnt, dest, h2s)


def _ffn_kernel(te_ref, nu_ref, xs_ref, w1_ref, b1_ref, w2_ref, b2_ref, o_ref, w1b_ref, w2b_ref, *, tr, dff):
    r = pl.program_id(0)

    @pl.when(jnp.logical_or(r == 0, te_ref[r] != te_ref[jnp.maximum(r - 1, 0)]))
    def _():
        w1b_ref[...] = w1_ref[0].astype(BF16)
        w2b_ref[...] = w2_ref[0].astype(BF16)

    @pl.when(r < nu_ref[0])
    def _():
        xs = _unpack_rows(xs_ref, tr).astype(BF16)
        hid = jnp.dot(xs, w1b_ref[...], preferred_element_type=F32) + b1_ref[0]
        glu = jnp.minimum(hid[:, :dff], SWIGLU_LIMIT)
        lin = jnp.clip(hid[:, dff:], -SWIGLU_LIMIT, SWIGLU_LIMIT)
        act = glu * _sigmoid(SWIGLU_ALPHA * glu) * (lin + 1.0)
        out = jnp.dot(act.astype(BF16), w2b_ref[...], preferred_element_type=F32) + b2_ref[0]
        _pack_rows(o_ref, out, tr)


def _ffn(tile_expert, n_used, xs, w1, b1, w2, b2, *, tr):
    n_e, d, dff2 = w1.shape
    n_tiles = tile_expert.shape[0]
    rows = lambda r, te, nu: (jnp.minimum(r, nu[0] - 1), 0)
    per_e = lambda a, b: pl.BlockSpec((1, a, b), lambda r, te, nu: (te[r], 0, 0))
    return pl.pallas_call(
        functools.partial(_ffn_kernel, tr=tr, dff=dff2 // 2),
        grid_spec=pltpu.PrefetchScalarGridSpec(
            num_scalar_prefetch=2, grid=(n_tiles,),
            in_specs=[pl.BlockSpec((tr * ROW_SLAB, LANES), rows), per_e(d, dff2), per_e(1, dff2), per_e(dff2 // 2, d),
                      per_e(1, d)],
            out_specs=pl.BlockSpec((tr * ROW_SLAB, LANES), rows),
            scratch_shapes=[pltpu.VMEM((d, dff2), BF16), pltpu.VMEM((dff2 // 2, d), BF16)]),
        out_shape=jax.ShapeDtypeStruct(xs.shape, jnp.uint32),
        compiler_params=_params(("arbitrary",)),
        name="moe_experts",
    )(tile_expert, n_used, xs, w1, b1, w2, b2)


def _combine_kernel(dest_hbm, ys_hbm, x1_ref, topw_ref, fg_ref, o_ref, dest_smem, buf_ref, sems, isem, *, tt):
    i = pl.program_id(0)
    n = pl.num_programs(0)

    def gather(tile):
        slot = tile % 2
        _dest_load(dest_hbm, dest_smem, isem, tile, slot, tt).wait()

        @pl.when(tile + 1 < n)
        def _():
            _dest_load(dest_hbm, dest_smem, isem, tile + 1, 1 - slot, tt).start()

        def issue(g, carry):
            toks = [g * ISSUE_GROUP + j for j in range(ISSUE_GROUP)]
            rows = [[dest_smem[(slot * SLAB + k) * tt + tok] for k in range(TOP_K)] for tok in toks]
            for m, (tok, row) in enumerate(zip(toks, rows)):
                for k in range(TOP_K):
                    _row_copy(ys_hbm, row[k], buf_ref.at[slot, k], tok, sems.at[slot]).start(
                        priority=(m * TOP_K + k) % 2)
            return carry
        lax.fori_loop(0, tt // ISSUE_GROUP, issue, 0)

    @pl.when(i == 0)
    def _():
        _dest_load(dest_hbm, dest_smem, isem, 0, 0, tt).start()
        gather(0)

    @pl.when(i + 1 < n)
    def _():
        gather(i + 1)

    slot = i % 2

    def drain(tok, carry):
        for k in range(TOP_K):
            _row_copy(ys_hbm, 0, buf_ref.at[slot, k], 0, sems.at[slot]).wait()
        return carry
    lax.fori_loop(0, tt, drain, 0, unroll=8)

    y = x1_ref[...]
    w = topw_ref[...]
    for k in range(TOP_K):
        y = y + w[:, k:k + 1] * _unpack_rows(buf_ref, tt, lead=(slot, k))
    ms = jnp.mean(y * y, axis=-1, keepdims=True)
    o_ref[...] = y * lax.rsqrt(ms + EPS) * fg_ref[...]


def _combine(dest, ys, x1, topw, fg, *, tt):
    t, d = x1.shape
    row = lambda n: pl.BlockSpec((tt, n), lambda i: (i, 0))
    anyspec = pl.BlockSpec(memory_space=pl.ANY)
    return pl.pallas_call(
        functools.partial(_combine_kernel, tt=tt),
        grid=(t // tt,),
        in_specs=[anyspec, anyspec, row(d), row(LANES), _resident((1, d))],
        out_specs=row(d),
        out_shape=jax.ShapeDtypeStruct((t, d), F32),
        scratch_shapes=[pltpu.SMEM((2 * SLAB * tt,), jnp.int32), pltpu.VMEM((2, TOP_K, tt * ROW_SLAB, LANES), jnp.uint32),
                        pltpu.SemaphoreType.DMA((2,)), pltpu.SemaphoreType.DMA],
        compiler_params=_params(("arbitrary",)),
        name="moe_combine",
    )(dest, ys, x1, topw, fg)


def _moe(x1, h2s, topi, topw, counts, w1, b1, w2, b2, fg, *, tt, tr):
    t = x1.shape[0]
    n_e = w1.shape[0]
    cnt = counts[:, 0, :n_e].astype(jnp.int32)
    nt = cnt.shape[0]
    total = jnp.sum(cnt, axis=0)
    padded = (total + tr - 1) // tr * tr
    e_ids = jnp.arange(n_e)
    ends = jnp.sum(jnp.where(e_ids[:, None] <= e_ids[None, :], padded[:, None], 0), axis=0)
    starts = ends - padded
    t_ids = jnp.arange(nt)
    before = jnp.sum(jnp.where((t_ids[:, None] < t_ids[None, :])[:, :, None], cnt[:, None, :], 0), axis=0)
    base = starts[None, :] + before
    base = jnp.pad(base.astype(F32), ((0, 0), (0, LANES - n_e)))[:, None, :]
    n_tiles = (t * TOP_K) // tr + n_e
    n_used = (ends[-1] // tr).astype(jnp.int32)
    r = jnp.minimum(jnp.arange(n_tiles, dtype=jnp.int32), n_used - 1)
    tile_expert = jnp.sum((ends // tr)[None, :] <= r[:, None], axis=1).astype(jnp.int32)

    dest = _plan(topi, base, tt=tt).reshape(-1)
    xs = _dispatch(starts + total, padded - total, dest, h2s, rows=n_tiles * tr, tt=tt, tr=tr)
    ys = _ffn(tile_expert, n_used.reshape(1), xs, w1, b1, w2, b2, tr=tr)
    return _combine(dest, ys, x1, topw, fg, tt=tt)


class _Tiles(NamedTuple):
    tokens: int
    attention: int
    conv: int
    expert_rows: int


def _tile(n, want):
    want = min(want, n)
    assert n % want == 0, (n, want)
    return want


def _choose_tiles(batch, seq):
    return _Tiles(tokens=_tile(batch * seq, 512), attention=_tile(seq, 256), conv=_tile(seq, 256),
                  expert_rows=_tile(batch * seq * TOP_K, 512))


def kernel(x, norm1_g, w_in, conv_dw_w, conv_dw_b, conv_norm_g, conv_norm_b, w_conv_out, w_attn_out, gate_b,
           w_out, norm2_g, router_w, router_b, expert_w1, expert_b1, expert_w2, expert_b2, final_norm_g):
    batch, seq, d = x.shape
    depth = w_in.shape[0]
    t = batch * seq
    assert d == 2 * ROW_SLAB * LANES and N_EXPERTS <= LANES
    assert depth == 1, "the final rms_norm is fused into the combine kernel of the only layer"
    row2 = lambda a: a.reshape(1, -1)
    l = 0

    tiles = _choose_tiles(batch, seq)
    x2 = x.reshape(t, d)
    u, q, k, v, gates = _in_proj(x2, row2(norm1_g[l]), w_in[l].astype(BF16), row2(gate_b[l]), tm=tiles.tokens)
    o = _attention(q, k, v, batch=batch, seq=seq, tb=tiles.attention, heads=ATTN_HEADS_PER_STEP)
    c = _conv_branch(u, conv_dw_w[l], row2(conv_dw_b[l]), row2(conv_norm_g[l]), row2(conv_norm_b[l]),
                     batch=batch, seq=seq, ts=tiles.conv)
    rw = jnp.pad(router_w[l], ((0, 0), (0, LANES - N_EXPERTS)))
    rw_hi = rw.astype(BF16)
    rw_lo = (rw - rw_hi.astype(F32)).astype(BF16)
    rb = jnp.pad(router_b[l].astype(F32), (0, LANES - N_EXPERTS), constant_values=-1e30).reshape(1, LANES)
    x1, h2s, topi, topw, counts = _mix(x2, c, o, gates, w_conv_out[l].astype(BF16), w_attn_out[l].astype(BF16),
                                       w_out[l].astype(BF16), row2(norm2_g[l]),
                                       jnp.concatenate([rw_hi, rw_lo], axis=1), rb, tm=tiles.tokens)
    out = _moe(x1, h2s, topi, topw, counts, expert_w1[l], expert_b1[l][:, None, :],
               expert_w2[l], expert_b2[l][:, None, :], row2(final_norm_g), tt=tiles.tokens, tr=tiles.expert_rows)
    return out.reshape(batch, seq, d)
```

```python
import functools
from typing import NamedTuple

import jax
import jax.numpy as jnp
from jax import lax
from jax.experimental import pallas as pl
from jax.experimental.pallas import tpu as pltpu

F32 = jnp.float32
BF16 = jnp.bfloat16

HEAD_DIM = 64
CONV_WIDTH = 31
N_EXPERTS = 32
TOP_K = 4
SWIGLU_LIMIT = 7.0
SWIGLU_ALPHA = 1.702
EPS = 1e-6

LANES = 128
SLAB = 8
ROW_SLAB = 4
HIGH_HALF = 0xFFFF0000
ATTN_HEADS_PER_STEP = 4
ATTN_TILES_PER_STEP = 4
ISSUE_GROUP = 8
CONV_HALO = 32
CONV_ROWS = 128
VMEM_LIMIT = 56 * 1024 * 1024

LOG2_E = 1.4426950408889634
SKIP_THRESHOLD = 150.0 * (1.0 + 2.0 ** -7)
SIGN_BIT = 0x80000000


def _params(sem):
    return pltpu.CompilerParams(dimension_semantics=sem, vmem_limit_bytes=VMEM_LIMIT)


def _resident(shape):
    return pl.BlockSpec(shape, lambda *_: (0,) * len(shape), pipeline_mode=pl.Buffered(1))


def _sigmoid(x):
    return 1.0 / (1.0 + jnp.exp(-x))


def _inproj_kernel(x_ref, g_ref, w_ref, gb_ref, u_ref, q_ref, k_ref, v_ref, gate_ref, *, d):
    x = x_ref[...]
    ms = jnp.mean(x * x, axis=-1, keepdims=True)
    h = (x * lax.rsqrt(ms + EPS) * g_ref[...]).astype(BF16)

    def proj(c):
        return jnp.dot(h, w_ref[:, c * d:(c + 1) * d], preferred_element_type=F32)

    u_ref[...] = (proj(0) * _sigmoid(proj(1))).astype(BF16)
    q_ref[...] = (proj(2) * (HEAD_DIM ** -0.5 * LOG2_E)).astype(BF16)
    k_ref[...] = proj(3).astype(BF16)
    v_ref[...] = proj(4).astype(BF16)
    gate_ref[:, :d] = _sigmoid(proj(5) + gb_ref[:, :d]).astype(BF16)
    gate_ref[:, d:] = _sigmoid(proj(6) + gb_ref[:, d:]).astype(BF16)


def _in_proj(x2, norm_g, w_in, gate_b, *, tm):
    t, d = x2.shape
    row = lambda n: pl.BlockSpec((tm, n), lambda i: (i, 0))
    out = lambda n: jax.ShapeDtypeStruct((t, n), BF16)
    return pl.pallas_call(
        functools.partial(_inproj_kernel, d=d),
        grid=(t // tm,),
        in_specs=[row(d), _resident((1, d)), _resident(w_in.shape), _resident((1, 2 * d))],
        out_specs=[row(d), row(d), row(d), row(d), row(2 * d)],
        out_shape=[out(d), out(d), out(d), out(d), out(2 * d)],
        compiler_params=_params(("parallel",)),
        name="in_proj",
    )(x2, norm_g, w_in, gate_b)


def _attn_tile(q_ref, k_ref, v_ref, o_ref, acc_ref, mass_ref, i, rows, *, tb, heads):
    r_iota = lax.broadcasted_iota(jnp.int32, (tb, tb), 0)
    c_iota = lax.broadcasted_iota(jnp.int32, (tb, tb), 1)
    upper = jnp.where(r_iota > c_iota, 1.0, 0.0).astype(BF16)
    causal = c_iota < r_iota

    first_head = lax.broadcasted_iota(jnp.int32, (tb, LANES), 1) < HEAD_DIM
    zeros = jnp.zeros((tb, LANES), BF16)

    def split_heads(x2):
        return jnp.concatenate([jnp.where(first_head, x2, zeros), jnp.where(first_head, zeros, x2)], axis=0)

    def pair_block(p, j, carry, acc, diagonal):
        lanes = slice(p * LANES, (p + 1) * LANES)
        start = pl.multiple_of(j * tb, tb)
        z = lax.dot_general(q_ref[rows, lanes], split_heads(k_ref[pl.ds(start, tb), lanes]),
                            (((1,), (1,)), ((), ())), preferred_element_type=F32)
        neg_abs = pltpu.bitcast(pltpu.bitcast(z, jnp.uint32) | jnp.uint32(SIGN_BIT), F32)
        l = jnp.log2(1.0 + jnp.exp2(neg_abs))
        sp = jnp.maximum(z, 0.0) + l
        logsig = z - sp
        if diagonal:
            sp = jnp.where(causal2, sp, 0.0)
        spb = sp.astype(BF16)
        prefix = jnp.dot(jnp.concatenate([spb[:, :tb], spb[:, tb:]], axis=0), upper,
                         preferred_element_type=F32)
        mass = (prefix[0:tb, 0:1] + sp[:, 0:1], prefix[tb:, 0:1] + sp[:, tb:tb + 1])
        later = jnp.concatenate([prefix[0:tb] + carry[0], prefix[tb:] + carry[1]], axis=1)
        a = jnp.exp2(logsig - later)
        if diagonal:
            a = jnp.where(causal2, a, 0.0)
        acc = acc + jnp.dot(a.astype(BF16), split_heads(v_ref[pl.ds(start, tb), lanes]),
                            preferred_element_type=F32)
        return (carry[0] + mass[0], carry[1] + mass[1]), acc

    pairs = heads * HEAD_DIM // LANES
    causal2 = jnp.concatenate([causal, causal], axis=1)

    def blocks(j, carries, accs, diagonal):
        new = [pair_block(p, j, carries[p], accs[p], diagonal) for p in range(pairs)]
        return tuple(c for c, _ in new), tuple(a for _, a in new)

    zero_carry = (jnp.zeros((tb, 1), F32), jnp.zeros((tb, 1), F32))

    def least_mass(carries):
        return jnp.min(functools.reduce(jnp.minimum, [c for pair in carries for c in pair]))

    def publish(carries, accs):
        for p in range(pairs):
            o_ref[rows, p * LANES:(p + 1) * LANES] = accs[p].astype(BF16)
            acc_ref[:, p * LANES:(p + 1) * LANES] = accs[p]
            for h in range(2):
                mass_ref[:, 2 * p + h:2 * p + h + 1] = carries[p][h]
        return least_mass(carries)

    def leading_blocks(with_previous):
        carries, accs = blocks(i, (zero_carry,) * pairs, (jnp.zeros((tb, LANES), F32),) * pairs, True)
        if with_previous:
            carries, accs = blocks(i - 1, carries, accs, False)
        return publish(carries, accs)

    least = lax.cond(i > 0, lambda: leading_blocks(True), lambda: leading_blocks(False))

    def cond(state):
        j, least = state
        return jnp.logical_and(j >= 0, least < SKIP_THRESHOLD)

    def body(state):
        j, _ = state
        carries = tuple((mass_ref[:, 2 * p:2 * p + 1], mass_ref[:, 2 * p + 1:2 * p + 2]) for p in range(pairs))
        accs = tuple(acc_ref[:, p * LANES:(p + 1) * LANES] for p in range(pairs))
        return j - 1, publish(*blocks(j, carries, accs, False))

    lax.while_loop(cond, body, (i - 2, least))


def _attn_kernel(q_ref, k_ref, v_ref, o_ref, acc_ref, mass_ref, *, tb, heads, tiles):
    step = pl.program_id(2)
    for n in range(tiles):
        _attn_tile(q_ref, k_ref, v_ref, o_ref, acc_ref, mass_ref, step * tiles + n, slice(n * tb, (n + 1) * tb),
                   tb=tb, heads=heads)


def _attention(q, k, v, *, batch, seq, tb, heads):
    t, d = q.shape
    tiles = _tile(seq // tb, ATTN_TILES_PER_STEP)
    nq = seq // (tb * tiles)
    width = heads * HEAD_DIM
    qspec = pl.BlockSpec((tb * tiles, width), lambda b, hp, i: (b * nq + i, hp))
    kvspec = pl.BlockSpec((seq, width), lambda b, hp, i: (b, hp))
    return pl.pallas_call(
        functools.partial(_attn_kernel, tb=tb, heads=heads, tiles=tiles),
        grid=(batch, d // width, nq),
        in_specs=[qspec, kvspec, kvspec],
        out_specs=qspec,
        out_shape=jax.ShapeDtypeStruct((t, d), BF16),
        scratch_shapes=[pltpu.VMEM((tb, width), F32), pltpu.VMEM((tb, LANES), F32)],
        compiler_params=_params(("parallel", "parallel", "arbitrary")),
        name="attention",
    )(q, k, v)


def _conv_kernel(prev_ref, cur_ref, w_ref, b_ref, g_ref, nb_ref, o_ref, win_ref, conv_ref, *, ts, d):
    i = pl.program_id(1)
    prev = prev_ref[...].astype(F32)
    win_ref[0:CONV_HALO, :] = jnp.where(i == 0, 0.0, prev)
    win_ref[CONV_HALO:, :] = cur_ref[...].astype(F32)
    base = CONV_HALO - (CONV_WIDTH - 1)
    rows = CONV_ROWS
    span = rows + CONV_HALO
    for c in range(d // LANES):
        lanes = slice(c * LANES, (c + 1) * LANES)
        for r0 in range(0, ts, rows):
            window = win_ref[r0:r0 + span, lanes]
            acc = jnp.zeros((rows, LANES), F32)
            for shift in range(SLAB):
                shifted = pltpu.roll(window, (span - shift) % span, axis=0) if shift else window
                for w in range(CONV_WIDTH):
                    if (base + w) % SLAB == shift:
                        lead = (base + w) // SLAB * SLAB
                        acc = acc + shifted[lead:lead + rows] * w_ref[w:w + 1, lanes]
            conv_ref[r0:r0 + rows, lanes] = acc + b_ref[:, lanes]
    y = conv_ref[...]
    mu = jnp.mean(y, axis=-1, keepdims=True)
    yc = y - mu
    var = jnp.mean(yc * yc, axis=-1, keepdims=True)
    yn = yc * lax.rsqrt(var + EPS) * g_ref[...] + nb_ref[...]
    o_ref[...] = (yn * _sigmoid(yn)).astype(BF16)


def _conv_branch(u, dw_w, dw_b, norm_g, norm_b, *, batch, seq, ts):
    t, d = u.shape
    ns = seq // ts
    per = ts // CONV_HALO
    cur = pl.BlockSpec((ts, d), lambda b, i: (b * ns + i, 0))
    prev = pl.BlockSpec((CONV_HALO, d), lambda b, i: (jnp.maximum((b * ns + i) * per - 1, 0), 0))
    return pl.pallas_call(
        functools.partial(_conv_kernel, ts=ts, d=d),
        grid=(batch, ns),
        in_specs=[prev, cur, _resident(dw_w.shape), _resident((1, d)), _resident((1, d)), _resident((1, d))],
        out_specs=cur,
        out_shape=jax.ShapeDtypeStruct((t, d), BF16),
        scratch_shapes=[pltpu.VMEM((ts + CONV_HALO, d), F32), pltpu.VMEM((ts, d), F32)],
        compiler_params=_params(("parallel", "parallel")),
        name="conv_branch",
    )(u, u, dw_w, dw_b, norm_g, norm_b)


def _pack_rows(ref, val, rows):
    half = val.shape[1] // 2
    rounded = val.astype(BF16).astype(F32)
    words = (lax.shift_right_logical(pltpu.bitcast(rounded[:, :half], jnp.uint32), jnp.uint32(16))
             | (pltpu.bitcast(rounded[:, half:], jnp.uint32) & jnp.uint32(HIGH_HALF)))
    for c in range(ROW_SLAB):
        ref[pl.ds(c, rows, stride=ROW_SLAB), :] = words[:, c * LANES:(c + 1) * LANES]


def _unpack_rows(ref, rows, lead=()):
    words = [ref[lead + (pl.ds(c, rows, stride=ROW_SLAB), slice(None))] for c in range(ROW_SLAB)]
    low = [pltpu.bitcast(lax.shift_left(w, jnp.uint32(16)), F32) for w in words]
    high = [pltpu.bitcast(w & jnp.uint32(HIGH_HALF), F32) for w in words]
    return jnp.concatenate(low + high, axis=1)


def _mix_kernel(x_ref, c_ref, o_ref, gate_ref, wc_ref, wa_ref, wo_ref, g2_ref, rw_ref, rb_ref,
                x1_ref, h2s_ref, topi_ref, topw_ref, cnt_ref, *, d, tm):
    y_conv = jnp.dot(c_ref[...], wc_ref[...], preferred_element_type=F32)
    y_attn = jnp.dot(o_ref[...], wa_ref[...], preferred_element_type=F32)
    m = gate_ref[:, :d].astype(F32) * y_conv + gate_ref[:, d:].astype(F32) * y_attn
    x1 = x_ref[...] + jnp.dot(m.astype(BF16), wo_ref[...], preferred_element_type=F32)
    x1_ref[...] = x1
    ms = jnp.mean(x1 * x1, axis=-1, keepdims=True)
    h2 = x1 * lax.rsqrt(ms + EPS) * g2_ref[...]
    h2_hi = h2.astype(BF16)
    _pack_rows(h2s_ref, h2, tm)
    h2_lo = (h2 - h2_hi.astype(F32)).astype(BF16)
    both = jnp.dot(h2_hi, rw_ref[...], preferred_element_type=F32)
    logits = (both[:, :LANES] + both[:, LANES:]
              + jnp.dot(h2_lo, rw_ref[:, :LANES], preferred_element_type=F32)
              + rb_ref[...])
    lane = lax.broadcasted_iota(jnp.int32, logits.shape, 1)
    work = logits
    top = None
    topi = jnp.zeros(logits.shape, jnp.int32)
    topw = jnp.zeros_like(logits)
    chosen = jnp.zeros_like(logits)
    for k in range(TOP_K):
        mx = jnp.max(work, axis=1, keepdims=True)
        idx = jnp.min(jnp.where(work == mx, lane, LANES), axis=1, keepdims=True)
        sel = lane == idx
        if k == 0:
            top = mx
        topi = jnp.where(lane == k, idx, topi)
        topw = jnp.where(lane == k, jnp.exp(mx - top), topw)
        chosen = jnp.where(sel, 1.0, chosen)
        work = jnp.where(sel, -jnp.inf, work)
    topi_ref[...] = topi
    topw_ref[...] = topw / jnp.sum(topw, axis=1, keepdims=True)
    cnt_ref[0] = jnp.sum(chosen, axis=0, keepdims=True)


def _mix(x2, c, o, gates, wc, wa, wo, g2, rw, rb, *, tm):
    t, d = x2.shape
    row = lambda n: pl.BlockSpec((tm, n), lambda i: (i, 0))
    return pl.pallas_call(
        functools.partial(_mix_kernel, d=d, tm=tm),
        grid=(t // tm,),
        in_specs=[row(d), row(d), row(d), row(2 * d), _resident((d, d)), _resident((d, d)), _resident((d, d)),
                  _resident((1, d)), _resident((d, 2 * LANES)), _resident((1, LANES))],
        out_specs=[row(d), pl.BlockSpec((tm * ROW_SLAB, LANES), lambda i: (i, 0)), row(LANES), row(LANES),
                   pl.BlockSpec((1, 1, LANES), lambda i: (i, 0, 0))],
        out_shape=[jax.ShapeDtypeStruct((t, d), F32), jax.ShapeDtypeStruct((t * ROW_SLAB, LANES), jnp.uint32),
                   jax.ShapeDtypeStruct((t, LANES), jnp.int32), jax.ShapeDtypeStruct((t, LANES), F32),
                   jax.ShapeDtypeStruct((t // tm, 1, LANES), F32)],
        compiler_params=_params(("parallel",)),
        name="mix_router",
    )(x2, c, o, gates, wc, wa, wo, g2, rw, rb)


def _plan_kernel(topi_ref, base_ref, dest_ref, *, tt):
    idx = topi_ref[...]
    lane = lax.broadcasted_iota(jnp.int32, idx.shape, 1)
    onehots = [lane == idx[:, k:k + 1] for k in range(TOP_K)]
    chosen = jnp.zeros(idx.shape, F32)
    for oh in onehots:
        chosen = jnp.where(oh, 1.0, chosen)
    r_iota = lax.broadcasted_iota(jnp.int32, (tt, tt), 0)
    c_iota = lax.broadcasted_iota(jnp.int32, (tt, tt), 1)
    earlier = jnp.where(c_iota < r_iota, 1.0, 0.0).astype(BF16)
    rank = jnp.dot(earlier, chosen.astype(BF16), preferred_element_type=F32)
    pos = rank + base_ref[0]
    dest = jnp.zeros(idx.shape, F32)
    for k, oh in enumerate(onehots):
        dk = jnp.sum(jnp.where(oh, pos, 0.0), axis=1, keepdims=True)
        dest = jnp.where(lane == k, dk, dest)
    dest_ref[...] = dest.T[0:SLAB, :].astype(jnp.int32)


def _plan(topi, base, *, tt):
    t = topi.shape[0]
    nt = t // tt
    return pl.pallas_call(
        functools.partial(_plan_kernel, tt=tt),
        grid=(nt,),
        in_specs=[pl.BlockSpec((tt, LANES), lambda i: (i, 0)), pl.BlockSpec((1, 1, LANES), lambda i: (i, 0, 0))],
        out_specs=pl.BlockSpec((SLAB, tt), lambda i: (i, 0)),
        out_shape=jax.ShapeDtypeStruct((nt * SLAB, tt), jnp.int32),
        compiler_params=_params(("parallel",)),
        name="moe_plan",
    )(topi, base)


def _row_copy(src_ref, src_row, dst_ref, dst_row, sem):
    return pltpu.make_async_copy(src_ref.at[pl.ds(pl.multiple_of(src_row * ROW_SLAB, ROW_SLAB), ROW_SLAB)],
                                 dst_ref.at[pl.ds(pl.multiple_of(dst_row * ROW_SLAB, ROW_SLAB), ROW_SLAB)], sem)


def _dest_load(dest_hbm, dest_smem, isem, tile, slot, tt):
    n = SLAB * tt
    return pltpu.make_async_copy(dest_hbm.at[pl.ds(pl.multiple_of(tile * n, n), n)],
                                 dest_smem.at[pl.ds(pl.multiple_of(slot * n, n), n)], isem)


def _pad_fill_copies(pad_start_ref, pad_cnt_ref, zero_ref, xs_hbm, sem, e, tr):
    cnt = pad_cnt_ref[e]
    out = []
    bit = 1
    while bit < tr:
        first = pad_start_ref[e] + (cnt & (bit - 1))
        copy = pltpu.make_async_copy(zero_ref.at[pl.ds(0, bit * ROW_SLAB)],
                                     xs_hbm.at[pl.ds(pl.multiple_of(first * ROW_SLAB, ROW_SLAB), bit * ROW_SLAB)], sem)
        out.append(((cnt & bit) != 0, copy))
        bit *= 2
    return out


def _dispatch_kernel(pad_start_ref, pad_cnt_ref, dest_hbm, h2s_ref, xs_hbm, dest_smem, zero_ref, sem, isem, zsem,
                     *, tt, tr, n_e):
    i = pl.program_id(0)
    slot = i % 2

    @pl.when(i == 0)
    def _():
        _dest_load(dest_hbm, dest_smem, isem, 0, 0, tt).start()

    @pl.when(i == 0)
    def _():
        zero_ref[...] = jnp.zeros_like(zero_ref)
        for e in range(n_e):
            for pred, copy in _pad_fill_copies(pad_start_ref, pad_cnt_ref, zero_ref, xs_hbm, zsem, e, tr):
                pl.when(pred)(copy.start)

    _dest_load(dest_hbm, dest_smem, isem, i, slot, tt).wait()

    @pl.when(i + 1 < pl.num_programs(0))
    def _():
        _dest_load(dest_hbm, dest_smem, isem, i + 1, 1 - slot, tt).start()

    def issue(g, carry):
        toks = [g * ISSUE_GROUP + j for j in range(ISSUE_GROUP)]
        rows = [[dest_smem[(slot * SLAB + k) * tt + tok] for k in range(TOP_K)] for tok in toks]
        for m, (tok, row) in enumerate(zip(toks, rows)):
            for k in range(TOP_K):
                _row_copy(h2s_ref, tok, xs_hbm, row[k], sem).start(priority=(m * TOP_K + k) % 2)
        return carry
    lax.fori_loop(0, tt // ISSUE_GROUP, issue, 0)

    def drain(tok, carry):
        for k in range(TOP_K):
            _row_copy(h2s_ref, 0, xs_hbm, 0, sem).wait()
        return carry
    lax.fori_loop(0, tt, drain, 0, unroll=8)

    @pl.when(i == pl.num_programs(0) - 1)
    def _():
        for e in range(n_e):
            for pred, copy in _pad_fill_copies(pad_start_ref, pad_cnt_ref, zero_ref, xs_hbm, zsem, e, tr):
                pl.when(pred)(copy.wait)


def _dispatch(pad_start, pad_cnt, dest, h2s, *, rows, tt, tr):
    nt = dest.shape[0] // (SLAB * tt)
    n_e = pad_start.shape[0]
    anyspec = pl.BlockSpec(memory_space=pl.ANY)
    return pl.pallas_call(
        functools.partial(_dispatch_kernel, tt=tt, tr=tr, n_e=n_e),
        grid_spec=pltpu.PrefetchScalarGridSpec(
            num_scalar_prefetch=2, grid=(nt,),
            in_specs=[anyspec, pl.BlockSpec((tt * ROW_SLAB, LANES), lambda i, ps, pc: (i, 0))], out_specs=anyspec,
            scratch_shapes=[pltpu.SMEM((2 * SLAB * tt,), jnp.int32), pltpu.VMEM((tr // 2 * ROW_SLAB, LANES), jnp.uint32),
                            pltpu.SemaphoreType.DMA, pltpu.SemaphoreType.DMA, pltpu.SemaphoreType.DMA]),
        out_shape=jax.ShapeDtypeStruct((rows * ROW_SLAB, LANES), jnp.uint32),
        compiler_params=_params(("arbitrary",)),
        name="moe_dispatch",
    )(pad_start, pad_cnt, dest, h2s)


def _ffn_kernel(te_ref, nu_ref, xs_ref, w1_ref, b1_ref, w2_ref, b2_ref, o_ref, w1b_ref, w2b_ref, *, tr, dff):
    r = pl.program_id(0)

    @pl.when(jnp.logical_or(r == 0, te_ref[r] != te_ref[jnp.maximum(r - 1, 0)]))
    def _():
        w1b_ref[...] = w1_ref[0].astype(BF16)
        w2b_ref[...] = w2_ref[0].astype(BF16)

    @pl.when(r < nu_ref[0])
    def _():
        xs = _unpack_rows(xs_ref, tr).astype(BF16)
        hid = jnp.dot(xs, w1b_ref[...], preferred_element_type=F32) + b1_ref[0]
        glu = jnp.minimum(hid[:, :dff], SWIGLU_LIMIT)
        lin = jnp.clip(hid[:, dff:], -SWIGLU_LIMIT, SWIGLU_LIMIT)
        act = glu * _sigmoid(SWIGLU_ALPHA * glu) * (lin + 1.0)
        out = jnp.dot(act.astype(BF16), w2b_ref[...], preferred_element_type=F32) + b2_ref[0]
        _pack_rows(o_ref, out, tr)


def _ffn(tile_expert, n_used, xs, w1, b1, w2, b2, *, tr):
    n_e, d, dff2 = w1.shape
    n_tiles = tile_expert.shape[0]
    rows = lambda r, te, nu: (jnp.minimum(r, nu[0] - 1), 0)
    per_e = lambda a, b: pl.BlockSpec((1, a, b), lambda r, te, nu: (te[r], 0, 0))
    return pl.pallas_call(
        functools.partial(_ffn_kernel, tr=tr, dff=dff2 // 2),
        grid_spec=pltpu.PrefetchScalarGridSpec(
            num_scalar_prefetch=2, grid=(n_tiles,),
            in_specs=[pl.BlockSpec((tr * ROW_SLAB, LANES), rows), per_e(d, dff2), per_e(1, dff2), per_e(dff2 // 2, d),
                      per_e(1, d)],
            out_specs=pl.BlockSpec((tr * ROW_SLAB, LANES), rows),
            scratch_shapes=[pltpu.VMEM((d, dff2), BF16), pltpu.VMEM((dff2 // 2, d), BF16)]),
        out_shape=jax.ShapeDtypeStruct(xs.shape, jnp.uint32),
        compiler_params=_params(("arbitrary",)),
        name="moe_experts",
    )(tile_expert, n_used, xs, w1, b1, w2, b2)


def _combine_kernel(dest_hbm, ys_hbm, x1_ref, topw_ref, fg_ref, o_ref, dest_smem, buf_ref, sems, isem, *, tt):
    i = pl.program_id(0)
    n = pl.num_programs(0)

    def gather(tile):
        slot = tile % 2
        _dest_load(dest_hbm, dest_smem, isem, tile, slot, tt).wait()

        @pl.when(tile + 1 < n)
        def _():
            _dest_load(dest_hbm, dest_smem, isem, tile + 1, 1 - slot, tt).start()

        def issue(g, carry):
            toks = [g * ISSUE_GROUP + j for j in range(ISSUE_GROUP)]
            rows = [[dest_smem[(slot * SLAB + k) * tt + tok] for k in range(TOP_K)] for tok in toks]
            for m, (tok, row) in enumerate(zip(toks, rows)):
                for k in range(TOP_K):
                    _row_copy(ys_hbm, row[k], buf_ref.at[slot, k], tok, sems.at[slot]).start(
                        priority=(m * TOP_K + k) % 2)
            return carry
        lax.fori_loop(0, tt // ISSUE_GROUP, issue, 0)

    @pl.when(i == 0)
    def _():
        _dest_load(dest_hbm, dest_smem, isem, 0, 0, tt).start()
        gather(0)

    @pl.when(i + 1 < n)
    def _():
        gather(i + 1)

    slot = i % 2

    def drain(tok, carry):
        for k in range(TOP_K):
            _row_copy(ys_hbm, 0, buf_ref.at[slot, k], 0, sems.at[slot]).wait()
        return carry
    lax.fori_loop(0, tt, drain, 0, unroll=8)

    y = x1_ref[...]
    w = topw_ref[...]
    for k in range(TOP_K):
        y = y + w[:, k:k + 1] * _unpack_rows(buf_ref, tt, lead=(slot, k))
    ms = jnp.mean(y * y, axis=-1, keepdims=True)
    o_ref[...] = y * lax.rsqrt(ms + EPS) * fg_ref[...]


def _combine(dest, ys, x1, topw, fg, *, tt):
    t, d = x1.shape
    row = lambda n: pl.BlockSpec((tt, n), lambda i: (i, 0))
    anyspec = pl.BlockSpec(memory_space=pl.ANY)
    return pl.pallas_call(
        functools.partial(_combine_kernel, tt=tt),
        grid=(t // tt,),
        in_specs=[anyspec, anyspec, row(d), row(LANES), _resident((1, d))],
        out_specs=row(d),
        out_shape=jax.ShapeDtypeStruct((t, d), F32),
        scratch_shapes=[pltpu.SMEM((2 * SLAB * tt,), jnp.int32), pltpu.VMEM((2, TOP_K, tt * ROW_SLAB, LANES), jnp.uint32),
                        pltpu.SemaphoreType.DMA((2,)), pltpu.SemaphoreType.DMA],
        compiler_params=_params(("arbitrary",)),
        name="moe_combine",
    )(dest, ys, x1, topw, fg)


def _moe(x1, h2s, topi, topw, counts, w1, b1, w2, b2, fg, *, tt, tr):
    t = x1.shape[0]
    n_e = w1.shape[0]
    cnt = counts[:, 0, :n_e].astype(jnp.int32)
    nt = cnt.shape[0]
    total = jnp.sum(cnt, axis=0)
    padded = (total + tr - 1) // tr * tr
    e_ids = jnp.arange(n_e)
    ends = jnp.sum(jnp.where(e_ids[:, None] <= e_ids[None, :], padded[:, None], 0), axis=0)
    starts = ends - padded
    t_ids = jnp.arange(nt)
    before = jnp.sum(jnp.where((t_ids[:, None] < t_ids[None, :])[:, :, None], cnt[:, None, :], 0), axis=0)
    base = starts[None, :] + before
    base = jnp.pad(base.astype(F32), ((0, 0), (0, LANES - n_e)))[:, None, :]
    n_tiles = (t * TOP_K) // tr + n_e
    n_used = (ends[-1] // tr).astype(jnp.int32)
    r = jnp.minimum(jnp.arange(n_tiles, dtype=jnp.int32), n_used - 1)
    tile_expert = jnp.sum((ends // tr)[None, :] <= r[:, None], axis=1).astype(jnp.int32)

    dest = _plan(topi, base, tt=tt).reshape(-1)
    xs = _dispatch(starts + total, padded - total, dest, h2s, rows=n_tiles * tr, tt=tt, tr=tr)
    ys = _ffn(tile_expert, n_used.reshape(1), xs, w1, b1, w2, b2, tr=tr)
    return _combine(dest, ys, x1, topw, fg, tt=tt)


class _Tiles(NamedTuple):
    tokens: int
    attention: int
    conv: int
    expert_rows: int


def _tile(n, want):
    want = min(want, n)
    assert n % want == 0, (n, want)
    return want


def _choose_tiles(batch, seq):
    return _Tiles(tokens=_tile(batch * seq, 512), attention=_tile(seq, 256), conv=_tile(seq, 256),
                  expert_rows=_tile(batch * seq * TOP_K, 512))


def kernel(x, norm1_g, w_in, conv_dw_w, conv_dw_b, conv_norm_g, conv_norm_b, w_conv_out, w_attn_out, gate_b,
           w_out, norm2_g, router_w, router_b, expert_w1, expert_b1, expert_w2, expert_b2, final_norm_g):
    batch, seq, d = x.shape
    depth = w_in.shape[0]
    t = batch * seq
    assert d == 2 * ROW_SLAB * LANES and N_EXPERTS <= LANES
    assert depth == 1, "the final rms_norm is fused into the combine kernel of the only layer"
    row2 = lambda a: a.reshape(1, -1)
    l = 0

    tiles = _choose_tiles(batch, seq)
    x2 = x.reshape(t, d)
    u, q, k, v, gates = _in_proj(x2, row2(norm1_g[l]), w_in[l].astype(BF16), row2(gate_b[l]), tm=tiles.tokens)
    o = _attention(q, k, v, batch=batch, seq=seq, tb=tiles.attention, heads=ATTN_HEADS_PER_STEP)
    c = _conv_branch(u, conv_dw_w[l], row2(conv_dw_b[l]), row2(conv_norm_g[l]), row2(conv_norm_b[l]),
                     batch=batch, seq=seq, ts=tiles.conv)
    rw = jnp.pad(router_w[l], ((0, 0), (0, LANES - N_EXPERTS)))
    rw_hi = rw.astype(BF16)
    rw_lo = (rw - rw_hi.astype(F32)).astype(BF16)
    rb = jnp.pad(router_b[l].astype(F32), (0, LANES - N_EXPERTS), constant_values=-1e30).reshape(1, LANES)
    x1, h2s, topi, topw, counts = _mix(x2, c, o, gates, w_conv_out[l].astype(BF16), w_attn_out[l].astype(BF16),
                                       w_out[l].astype(BF16), row2(norm2_g[l]),
                                       jnp.concatenate([rw_hi, rw_lo], axis=1), rb, tm=tiles.tokens)
    out = _moe(x1, h2s, topi, topw, counts, expert_w1[l], expert_b1[l][:, None, :],
               expert_w2[l], expert_b2[l][:, None, :], row2(final_norm_g), tt=tiles.tokens, tr=tiles.expert_rows)
    return out.reshape(batch, seq, d)
```
